```python
import jax, jax.numpy as jnp
from jax import lax
import numpy as np

D_MODEL = 1024
BATCH = 2
SEQ = 16384
DEPTH = 1

ATT_HEADS = 8
ATT_KV_HEADS = 2
ATT_HEAD_DIM = 64
WINDOW = 128
ATT_BLOCK = 128
ROPE_THETA = 10000.0
MLSTM_HEADS = 4
MLSTM_HEAD_DIM = 128
MLSTM_CHUNK = 128
CONV_WIDTH = 4
ATT_Q_W = ATT_HEADS * ATT_HEAD_DIM
ATT_KV_W = ATT_KV_HEADS * ATT_HEAD_DIM
MLSTM_W = MLSTM_HEADS * MLSTM_HEAD_DIM
N_BRANCHES = 2
IN_SPLITS = (ATT_Q_W, ATT_KV_W, ATT_KV_W, MLSTM_W, MLSTM_W, MLSTM_W, MLSTM_W,
             MLSTM_HEADS, MLSTM_HEADS, D_MODEL, D_MODEL)
IN_PROJ_W = sum(IN_SPLITS)
PEER_HEADS = 8
PEER_N_KEYS = 128
PEER_N_EXPERTS = PEER_N_KEYS * PEER_N_KEYS
PEER_QUERY_DIM = 256
PEER_HALF = PEER_QUERY_DIM // 2
PEER_TOPK = 16
PEER_BLOCK = 128
EPS = 1e-6

kernel_name = 'hybrid_swa_mlstm_peer_adaln'

F32 = jnp.float32


def rmsnorm(x, w):
    xf = x.astype(F32)
    y = xf * lax.rsqrt(jnp.mean(xf * xf, axis=-1, keepdims=True) + EPS)
    return (y * w.astype(F32)).astype(x.dtype)


def split_cols(t, sizes):
    offs, acc = [], 0
    for s_ in sizes[:-1]:
        acc += s_
        offs.append(acc)
    return jnp.split(t, offs, axis=-1)


def rope(x):
    s, hd = x.shape[1], x.shape[-1]
    half = hd // 2
    inv_freq = ROPE_THETA ** (-jnp.arange(half, dtype=F32) * 2.0 / hd)
    ang = jnp.arange(s, dtype=F32)[:, None] * inv_freq[None, :]
    cos, sin = jnp.cos(ang)[:, None, :], jnp.sin(ang)[:, None, :]
    xf = x.astype(F32)
    x1, x2 = xf[..., :half], xf[..., half:]
    return jnp.concatenate([x1 * cos - x2 * sin, x2 * cos + x1 * sin], axis=-1)


def sliding_window_attention(q, k, v, sinks):
    b, s, _, hd = q.shape
    nb = s // ATT_BLOCK
    g = ATT_HEADS // ATT_KV_HEADS
    qb = q.astype(F32).reshape(b, nb, ATT_BLOCK, ATT_KV_HEADS, g, hd)
    kb = k.astype(F32).reshape(b, nb, ATT_BLOCK, ATT_KV_HEADS, hd)
    vb = v.astype(F32).reshape(b, nb, ATT_BLOCK, ATT_KV_HEADS, hd)

    def with_prev(t):
        prev = jnp.pad(t, ((0, 0), (1, 0), (0, 0), (0, 0), (0, 0)))[:, :-1]
        return jnp.concatenate([prev, t], axis=2)

    kk, vv = with_prev(kb), with_prev(vb)
    scores = jnp.einsum('bnqhgd,bnkhd->bnhgqk', qb, kk) * (hd ** -0.5)
    qi = jnp.arange(ATT_BLOCK)[:, None]
    ki = jnp.arange(2 * ATT_BLOCK)[None, :]
    rel = qi + ATT_BLOCK - ki
    key_pos = jnp.arange(nb)[:, None, None] * ATT_BLOCK - ATT_BLOCK + ki[None]
    mask = (rel >= 0) & (rel < WINDOW) & (key_pos >= 0)
    scores = jnp.where(mask[None, :, None, None], scores, -jnp.inf)
    sink = sinks.astype(F32).reshape(1, 1, ATT_KV_HEADS, g, 1)
    m = jnp.maximum(scores.max(-1), sink)
    p = jnp.exp(scores - m[..., None])
    denom = p.sum(-1) + jnp.exp(sink - m)
    p = p / denom[..., None]
    out = jnp.einsum('bnhgqk,bnkhd->bnqhgd', p, vv)
    return out.reshape(b, s, ATT_HEADS * hd)


def causal_depthwise_conv(x, w):
    ch = x.shape[-1]
    return lax.conv_general_dilated(
        x, w[:, None, :].astype(x.dtype), window_strides=(1,),
        padding=[(CONV_WIDTH - 1, 0)], dimension_numbers=('NWC', 'WIO', 'NWC'),
        feature_group_count=ch)


def mlstm_chunkwise(q, k, v, i_pre, f_pre):
    b, s, h, d = q.shape
    L = MLSTM_CHUNK
    nc = s // L

    def to_chunks(t):
        t = t.astype(F32).reshape((b, nc, L, h) + t.shape[3:])
        return jnp.moveaxis(t, 3, 1)

    qc, kc, vc = to_chunks(q), to_chunks(k) * (d ** -0.5), to_chunks(v)
    ic = to_chunks(i_pre)
    lf = jax.nn.log_sigmoid(to_chunks(f_pre))
    bcum = jnp.cumsum(lf, axis=-1)
    g_tot = bcum[..., -1]
    a = g_tot[..., None] - bcum + ic

    def step(carry, xs):
        C, n, m = carry
        k_c, v_c, a_c, g_c = xs
        m_new = jnp.maximum(g_c + m, a_c.max(-1))
        decay = jnp.exp(g_c + m - m_new)
        wk = jnp.exp(a_c - m_new[..., None])[..., None] * k_c
        C_new = decay[..., None, None] * C + jnp.einsum('bhld,bhle->bhde', wk, v_c)
        n_new = decay[..., None] * n + wk.sum(axis=2)
        return (C_new, n_new, m_new), (C, n, m)

    init = (jnp.zeros((b, h, d, d), F32), jnp.zeros((b, h, d), F32), jnp.zeros((b, h), F32))
    xs = (jnp.moveaxis(kc, 2, 0), jnp.moveaxis(vc, 2, 0),
          jnp.moveaxis(a, 2, 0), jnp.moveaxis(g_tot, 2, 0))
    _, (C_st, n_st, m_st) = lax.scan(step, init, xs)
    C_st = jnp.moveaxis(C_st, 0, 2)
    n_st = jnp.moveaxis(n_st, 0, 2)
    m_st = jnp.moveaxis(m_st, 0, 2)

    causal = jnp.tril(jnp.ones((L, L), dtype=bool))
    D = bcum[..., :, None] - bcum[..., None, :] + ic[..., None, :]
    D = jnp.where(causal, D, -jnp.inf)
    m_inter = bcum + m_st[..., None]
    m_out = jnp.maximum(m_inter, D.max(-1))
    w_intra = jnp.exp(D - m_out[..., None]) * jnp.einsum('bhcld,bhcsd->bhcls', qc, kc)
    w_inter = jnp.exp(m_inter - m_out)
    num = (jnp.einsum('bhcls,bhcsd->bhcld', w_intra, vc)
           + w_inter[..., None] * jnp.einsum('bhcld,bhcde->bhcle', qc, C_st))
    den = w_intra.sum(-1) + w_inter * jnp.einsum('bhcld,bhcd->bhcl', qc, n_st)
    hid = num / jnp.maximum(jnp.abs(den), jnp.exp(-m_out))[..., None]
    return jnp.moveaxis(hid, 1, 3).reshape(b, s, h, d)


def token_mixer(u, w_in, conv_w, att_sinks, i_bias, f_bias, mlstm_norm_w,
                w_att_branch, w_mlstm_branch, w_out):
    b, s, _ = u.shape
    proj = u @ w_in
    aq, ak, av, mq, mk, mv, mo, mi, mf, ga, gm = split_cols(proj, IN_SPLITS)
    aq = rope(aq.reshape(b, s, ATT_HEADS, ATT_HEAD_DIM))
    ak = rope(ak.reshape(b, s, ATT_KV_HEADS, ATT_HEAD_DIM))
    av = av.reshape(b, s, ATT_KV_HEADS, ATT_HEAD_DIM)
    att = sliding_window_attention(aq, ak, av, att_sinks).astype(u.dtype)
    mqk = jax.nn.silu(causal_depthwise_conv(jnp.concatenate([mq, mk], axis=-1), conv_w))
    mq, mk = jnp.split(mqk, 2, axis=-1)
    shp = (b, s, MLSTM_HEADS, MLSTM_HEAD_DIM)
    hid = mlstm_chunkwise(mq.reshape(shp), mk.reshape(shp), mv.reshape(shp),
                          mi.astype(F32) + i_bias.astype(F32), mf.astype(F32) + f_bias.astype(F32))
    hid = jax.nn.sigmoid(mo.astype(F32)).reshape(shp) * hid
    hid = hid * lax.rsqrt(jnp.mean(hid * hid, axis=-1, keepdims=True) + EPS)
    mls = (hid.reshape(b, s, MLSTM_W) * mlstm_norm_w.astype(F32)).astype(u.dtype)
    merged = jax.nn.sigmoid(ga) * (att @ w_att_branch) + jax.nn.sigmoid(gm) * (mls @ w_mlstm_branch)
    return merged @ w_out


def peer_ffn(u, w_query, sub_keys, exp_u, exp_v):
    b, s, d = u.shape
    tokens = u.reshape(-1, PEER_BLOCK, d)

    def block(xb):
        t = xb.shape[0]
        qy = (xb @ w_query).reshape(t, PEER_HEADS, 2, PEER_HALF).astype(F32)
        scores = jnp.einsum('thpd,pnd->thpn', qy, sub_keys.astype(F32))
        top_s, top_i = lax.top_k(scores, PEER_TOPK)
        cand_s = top_s[:, :, 0, :, None] + top_s[:, :, 1, None, :]
        cand_i = top_i[:, :, 0, :, None] * PEER_N_KEYS + top_i[:, :, 1, None, :]
        kk = PEER_TOPK * PEER_TOPK
        best_s, best_j = lax.top_k(cand_s.reshape(t, PEER_HEADS, kk), PEER_TOPK)
        idx = jnp.take_along_axis(cand_i.reshape(t, PEER_HEADS, kk), best_j, axis=-1)
        gate = jax.nn.softmax(best_s, axis=-1)
        u_sel = jnp.take(exp_u, idx, axis=0)
        v_sel = jnp.take(exp_v, idx, axis=0)
        act = jax.nn.gelu(jnp.einsum('td,thkd->thk', xb, u_sel).astype(F32), approximate=False)
        return jnp.einsum('thk,thkd->td', (gate * act).astype(xb.dtype), v_sel)

    return lax.map(block, tokens).reshape(b, s, d)


def setup_inputs(seed: int = 0) -> dict:
    key = jax.random.key(seed)
    ks = jax.random.split(key, 24)

    def nrm(k, shape, scale):
        return jax.random.normal(k, shape, F32) * scale

    return {
        'x': nrm(ks[0], (BATCH, SEQ, D_MODEL), 1.0),
        'c': nrm(ks[1], (BATCH, D_MODEL), 1.0),
        'w_ada': nrm(ks[2], (DEPTH, D_MODEL, 6 * D_MODEL), 0.5 * D_MODEL ** -0.5),
        'b_ada': nrm(ks[3], (DEPTH, 6 * D_MODEL), 0.02),
        'norm1_w': 1.0 + nrm(ks[4], (DEPTH, D_MODEL), 0.02),
        'w_in': nrm(ks[5], (DEPTH, D_MODEL, IN_PROJ_W), D_MODEL ** -0.5),
        'conv_w': nrm(ks[6], (DEPTH, CONV_WIDTH, 2 * MLSTM_W), CONV_WIDTH ** -0.5),
        'att_sinks': nrm(ks[7], (DEPTH, ATT_HEADS), 1.0),
        'i_bias': nrm(ks[8], (DEPTH, MLSTM_HEADS), 0.1),
        'f_bias': jnp.linspace(3.0, 6.0, MLSTM_HEADS, dtype=F32)[None, :]
                  + nrm(ks[9], (DEPTH, MLSTM_HEADS), 0.1),
        'mlstm_norm_w': 1.0 + nrm(ks[10], (DEPTH, MLSTM_W), 0.02),
        'w_att_branch': nrm(ks[11], (DEPTH, ATT_Q_W, D_MODEL), ATT_Q_W ** -0.5),
        'w_mlstm_branch': nrm(ks[12], (DEPTH, MLSTM_W, D_MODEL), MLSTM_W ** -0.5),
        'w_out': nrm(ks[13], (DEPTH, D_MODEL, D_MODEL), D_MODEL ** -0.5),
        'norm2_w': 1.0 + nrm(ks[14], (DEPTH, D_MODEL), 0.02),
        'peer_w_query': nrm(ks[15], (DEPTH, D_MODEL, PEER_HEADS * PEER_QUERY_DIM), D_MODEL ** -0.5),
        'peer_sub_keys': nrm(ks[16], (DEPTH, 2, PEER_N_KEYS, PEER_HALF), PEER_HALF ** -0.5),
        'peer_u': nrm(ks[17], (DEPTH, PEER_N_EXPERTS, D_MODEL), D_MODEL ** -0.5),
        'peer_v': nrm(ks[18], (DEPTH, PEER_N_EXPERTS, D_MODEL), PEER_HEADS ** -0.5),
        'norm_f_w': 1.0 + nrm(ks[19], (D_MODEL,), 0.02),
    }


def reference(x, c, w_ada, b_ada, norm1_w, w_in, conv_w, att_sinks, i_bias, f_bias,
              mlstm_norm_w, w_att_branch, w_mlstm_branch, w_out, norm2_w,
              peer_w_query, peer_sub_keys, peer_u, peer_v, norm_f_w):
    h = x
    c_act = jax.nn.silu(c)
    for l in range(DEPTH):
        mod = (c_act @ w_ada[l] + b_ada[l])[:, None, :]
        sh1, sc1, g1, sh2, sc2, g2 = jnp.split(mod, 6, axis=-1)
        u = rmsnorm(h, norm1_w[l]) * (1 + sc1) + sh1
        h = h + g1 * token_mixer(u, w_in[l], conv_w[l], att_sinks[l], i_bias[l], f_bias[l],
                                 mlstm_norm_w[l], w_att_branch[l], w_mlstm_branch[l], w_out[l])
        u2 = rmsnorm(h, norm2_w[l]) * (1 + sc2) + sh2
        h = h + g2 * peer_ffn(u2, peer_w_query[l], peer_sub_keys[l], peer_u[l], peer_v[l])
    return rmsnorm(h, norm_f_w)
```

```python
import functools
import math

import numpy as np
import jax
import jax.numpy as jnp
from jax import lax
from jax.experimental import pallas as pl
from jax.experimental.pallas import tpu as pltpu

F32 = jnp.float32
BF16 = jnp.bfloat16
HIGHEST = lax.Precision.HIGHEST

D_MODEL = 1024
ATT_HEADS = 8
ATT_KV_HEADS = 2
ATT_HEAD_DIM = 64
ATT_BLOCK = 128
ROPE_THETA = 10000.0
MLSTM_HEADS = 4
MLSTM_HEAD_DIM = 128
MLSTM_CHUNK = 128
CONV_WIDTH = 4
ATT_Q_W = ATT_HEADS * ATT_HEAD_DIM
ATT_KV_W = ATT_KV_HEADS * ATT_HEAD_DIM
MLSTM_W = MLSTM_HEADS * MLSTM_HEAD_DIM
PEER_HEADS = 8
PEER_N_KEYS = 128
PEER_HALF = 128
PEER_TOPK = 16
EPS = 1e-6

LANES = 128
SUBLANES = 8
VMEM_LIMIT = 48 * 1024 * 1024

ATT_COLS = ATT_Q_W + 2 * ATT_KV_W
MQK_COLS = 2 * MLSTM_W
GATE_COLS = 2 * D_MODEL
IF_COLS = LANES
OFF_ATT = 0
OFF_MQK = OFF_ATT + ATT_COLS
OFF_MV = OFF_MQK + MQK_COLS
OFF_MO = OFF_MV + MLSTM_W
OFF_G = OFF_MO + MLSTM_W
OFF_IF = OFF_G + GATE_COLS
IN_COLS = OFF_IF + IF_COLS

TOK_TILE = 512
PEER_TOK = 512
PEER_IBLK = 8
NEG_INF = float("-inf")


def _cparams(sem):
    return pltpu.CompilerParams(dimension_semantics=sem, vmem_limit_bytes=VMEM_LIMIT)


def _rmsnorm_rows(x, w):
    return x * lax.rsqrt(jnp.mean(x * x, axis=-1, keepdims=True) + EPS) * w


def _ada_kernel(c_ref, w_ref, b_ref, o_ref):
    c = c_ref[...]
    act = c * jax.nn.sigmoid(c)
    o_ref[...] = jnp.dot(act, w_ref[...], preferred_element_type=F32, precision=HIGHEST) + b_ref[...]


def _ada(c, w, b):
    bsz, d = c.shape
    n_out = w.shape[1]
    return pl.pallas_call(
        _ada_kernel,
        grid=(n_out // d,),
        in_specs=[pl.BlockSpec((bsz, d), lambda j: (0, 0)),
                  pl.BlockSpec((d, d), lambda j: (0, j)),
                  pl.BlockSpec((1, d), lambda j: (0, j))],
        out_specs=pl.BlockSpec((bsz, d), lambda j: (0, j)),
        out_shape=jax.ShapeDtypeStruct((bsz, n_out), F32),
        compiler_params=_cparams(("arbitrary",)),
        name="ada",
    )(c, w, b.reshape(1, n_out))


def _inproj_kernel(x_ref, nw_ref, sc_ref, sh_ref, w_ref,
                   att_ref, mqk_ref, mv_ref, mo_ref, g_ref, if_ref):
    y = _rmsnorm_rows(x_ref[...], nw_ref[...])
    u = (y * (1.0 + sc_ref[...]) + sh_ref[...]).astype(BF16)

    def proj(lo, width):
        return jnp.dot(u, w_ref[:, lo:lo + width], preferred_element_type=F32)

    att_ref[...] = proj(OFF_ATT, ATT_COLS).astype(BF16)
    mqk_ref[...] = proj(OFF_MQK, MQK_COLS).astype(BF16)
    mv_ref[...] = proj(OFF_MV, MLSTM_W).astype(BF16)
    mo_ref[...] = proj(OFF_MO, MLSTM_W).astype(BF16)
    g_ref[...] = proj(OFF_G, GATE_COLS).astype(BF16)
    if_ref[...] = proj(OFF_IF, IF_COLS)


def _inproj(x, norm_w, sc, sh, w_perm):
    bsz, s, d = x.shape
    tm = TOK_TILE
    tok = lambda width: pl.BlockSpec((None, tm, width), lambda b, i: (b, i, 0))
    vec = pl.BlockSpec((None, 1, d), lambda b, i: (b, 0, 0))
    widths = (ATT_COLS, MQK_COLS, MLSTM_W, MLSTM_W, GATE_COLS, IF_COLS)
    dtypes = (BF16, BF16, BF16, BF16, BF16, F32)
    return pl.pallas_call(
        _inproj_kernel,
        grid=(bsz, s // tm),
        in_specs=[tok(d),
                  pl.BlockSpec((1, d), lambda b, i: (0, 0)),
                  vec, vec,
                  pl.BlockSpec((d, IN_COLS), lambda b, i: (0, 0))],
        out_specs=[tok(w) for w in widths],
        out_shape=[jax.ShapeDtypeStruct((bsz, s, w), dt) for w, dt in zip(widths, dtypes)],
        compiler_params=_cparams(("parallel", "parallel")),
        name="inproj",
    )(x, norm_w.reshape(1, d), sc, sh, w_perm)


def _attn_kernel(sink_ref, cur_ref, prev_ref, cosc_ref, sinc_ref, cosp_ref, sinp_ref, o_ref):
    blk = ATT_BLOCK
    n = pl.program_id(1)
    lane = lax.broadcasted_iota(jnp.int32, (blk, LANES), 1)
    first_half = (lane & (ATT_HEAD_DIM - 1)) < (ATT_HEAD_DIM // 2)
    low = lane < ATT_HEAD_DIM

    def rope(xf, cos, sin):
        rot = jnp.where(first_half, pltpu.roll(xf, LANES - ATT_HEAD_DIM // 2, 1),
                        pltpu.roll(xf, ATT_HEAD_DIM // 2, 1))
        return xf * cos + rot * sin

    def dup(xf, g):
        sw = pltpu.roll(xf, ATT_HEAD_DIM, 1)
        return jnp.where(low, xf, sw) if g == 0 else jnp.where(low, sw, xf)

    cosc, sinc = cosc_ref[...], sinc_ref[...]
    cosp, sinp = cosp_ref[...], sinp_ref[...]
    k_cur = rope(cur_ref[:, ATT_Q_W:ATT_Q_W + ATT_KV_W].astype(F32), cosc, sinc)
    k_prev = rope(prev_ref[:, ATT_Q_W:ATT_Q_W + ATT_KV_W].astype(F32), cosp, sinp)
    v_cur = cur_ref[:, ATT_Q_W + ATT_KV_W:ATT_COLS].astype(F32)
    v_prev = prev_ref[:, ATT_Q_W + ATT_KV_W:ATT_COLS].astype(F32)

    qi = lax.broadcasted_iota(jnp.int32, (blk, 2 * blk), 0)
    ki = lax.broadcasted_iota(jnp.int32, (blk, 2 * blk), 1)
    q_lim = jnp.where(n > 0, qi, 2 * blk)
    mask = ((ki < blk) & (ki > q_lim)) | ((ki >= blk) & ((ki - blk) <= qi))

    scale = ATT_HEAD_DIM ** -0.5
    group = ATT_HEADS // ATT_KV_HEADS
    for g in range(ATT_KV_HEADS):
        kk = jnp.concatenate([dup(k_prev, g), dup(k_cur, g)], axis=0).astype(BF16)
        vv = jnp.concatenate([dup(v_prev, g), dup(v_cur, g)], axis=0).astype(BF16)
        for pair in range(group // 2):
            j = g * (group // 2) + pair
            qp = rope(cur_ref[:, j * LANES:(j + 1) * LANES].astype(F32), cosc, sinc) * scale
            outs = []
            for half in range(2):
                head = 2 * j + half
                qh = jnp.where(low, qp, 0.0) if half == 0 else jnp.where(low, 0.0, qp)
                s = lax.dot_general(qh.astype(BF16), kk, (((1,), (1,)), ((), ())),
                                    preferred_element_type=F32)
                s = jnp.where(mask, s, NEG_INF)
                sink = sink_ref[head]
                m = jnp.maximum(jnp.max(s, axis=-1, keepdims=True), sink)
                p = jnp.exp(s - m)
                denom = jnp.sum(p, axis=-1, keepdims=True) + jnp.exp(sink - m)
                o = jnp.dot(p.astype(BF16), vv, preferred_element_type=F32)
                outs.append(o / denom)
            o_ref[:, j * LANES:(j + 1) * LANES] = jnp.where(low, outs[0], outs[1]).astype(BF16)


def _attention(att_in, sinks, cos_t, sin_t):
    bsz, s, _ = att_in.shape
    blk = ATT_BLOCK
    nb = s // blk
    prev = lambda n: jnp.maximum(n - 1, 0)
    return pl.pallas_call(
        _attn_kernel,
        grid=(bsz, nb),
        in_specs=[pl.BlockSpec(memory_space=pltpu.SMEM),
                  pl.BlockSpec((None, blk, ATT_COLS), lambda b, n: (b, n, 0)),
                  pl.BlockSpec((None, blk, ATT_COLS), lambda b, n: (b, prev(n), 0)),
                  pl.BlockSpec((blk, LANES), lambda b, n: (n, 0)),
                  pl.BlockSpec((blk, LANES), lambda b, n: (n, 0)),
                  pl.BlockSpec((blk, LANES), lambda b, n: (prev(n), 0)),
                  pl.BlockSpec((blk, LANES), lambda b, n: (prev(n), 0))],
        out_specs=pl.BlockSpec((None, blk, ATT_Q_W), lambda b, n: (b, n, 0)),
        out_shape=jax.ShapeDtypeStruct((bsz, s, ATT_Q_W), BF16),
        compiler_params=_cparams(("parallel", "parallel")),
        name="attn",
    )(sinks, att_in, att_in, cos_t, sin_t, cos_t, sin_t)


def _mlstm_kernel(mqk_ref, mv_ref, mo_ref, if_ref, convw_ref, bias_ref, nw_ref,
                  o_ref, buf_ref, c_ref, n_ref, m_ref):
    L = MLSTM_CHUNK
    d = MLSTM_HEAD_DIM
    tail = SUBLANES

    @pl.when(pl.program_id(1) == 0)
    def _():
        buf_ref[0:tail, :] = jnp.zeros((tail, 2 * MLSTM_W), F32)
        c_ref[...] = jnp.zeros_like(c_ref)
        n_ref[...] = jnp.zeros_like(n_ref)
        m_ref[...] = jnp.zeros_like(m_ref)

    buf_ref[tail:tail + L, :] = mqk_ref[...].astype(F32)
    y = jnp.zeros((L, 2 * MLSTM_W), F32)
    for j in range(CONV_WIDTH):
        off = tail - (CONV_WIDTH - 1) + j
        y = y + convw_ref[j:j + 1, :] * buf_ref[off:off + L, :]
    buf_ref[0:tail, :] = buf_ref[L:L + tail, :]
    y = y * jax.nn.sigmoid(y)
    q_all = y[:, :MLSTM_W]
    k_all = y[:, MLSTM_W:] * (d ** -0.5)

    gates = if_ref[...] + bias_ref[...]
    lf = jnp.minimum(gates, 0.0) - jnp.log1p(jnp.exp(-jnp.abs(gates)))
    row = lax.broadcasted_iota(jnp.int32, (L, L), 0)
    col = lax.broadcasted_iota(jnp.int32, (L, L), 1)
    causal = col <= row
    tril = causal.astype(F32)
    triu = (row <= col).astype(F32)
    bcum_c = jnp.dot(tril, lf, preferred_element_type=F32, precision=HIGHEST)
    gates_t = gates.T
    bcum_r = jnp.dot(lf.T, triu, preferred_element_type=F32, precision=HIGHEST)

    for h in range(MLSTM_HEADS):
        hs = slice(h * d, (h + 1) * d)
        fl = MLSTM_HEADS + h
        q = q_all[:, hs]
        k = k_all[:, hs]
        v = mv_ref[:, hs]
        ic_c = gates[:, h:h + 1]
        ic_r = gates_t[h:h + 1, :]
        b_c = bcum_c[:, fl:fl + 1]
        b_r = bcum_r[fl:fl + 1, :]
        g_tot = b_c[L - 1:L, :]
        c_prev = c_ref[h]
        n_prev = n_ref[h:h + 1, :]
        m_prev = m_ref[h:h + 1, 0:1]

        dmat = jnp.where(causal, b_c - b_r + ic_r, NEG_INF)
        m_inter = b_c + m_prev
        m_out = jnp.maximum(m_inter, jnp.max(dmat, axis=-1, keepdims=True))
        qb = q.astype(BF16)
        kb = k.astype(BF16)
        qk = lax.dot_general(qb, kb, (((1,), (1,)), ((), ())), preferred_element_type=F32)
        w_intra = jnp.exp(dmat - m_out) * qk
        w_inter = jnp.exp(m_inter - m_out)
        num = (jnp.dot(w_intra.astype(BF16), v, preferred_element_type=F32)
               + w_inter * jnp.dot(qb, c_prev.astype(BF16), preferred_element_type=F32))
        den = (jnp.sum(w_intra, axis=-1, keepdims=True)
               + w_inter * jnp.sum(q * n_prev, axis=-1, keepdims=True))
        hid = num / jnp.maximum(jnp.abs(den), jnp.exp(-m_out))
        hid = jax.nn.sigmoid(mo_ref[:, hs].astype(F32)) * hid
        hid = hid * lax.rsqrt(jnp.mean(hid * hid, axis=-1, keepdims=True) + EPS)
        o_ref[:, hs] = (hid * nw_ref[:, hs]).astype(BF16)

        a_c = g_tot - b_c + ic_c
        m_new = jnp.maximum(g_tot + m_prev, jnp.max(a_c, axis=0, keepdims=True))
        decay = jnp.exp(g_tot + m_prev - m_new)
        wk = jnp.exp(a_c - m_new) * k
        c_ref[h] = decay * c_prev + lax.dot_general(
            wk.astype(BF16), v, (((0,), (0,)), ((), ())), preferred_element_type=F32)
        n_ref[h:h + 1, :] = decay * n_prev + jnp.sum(wk, axis=0, keepdims=True)
        m_ref[h:h + 1, :] = jnp.broadcast_to(m_new, (1, LANES))


def _mlstm(mqk, mv, mo, gif, conv_w, bias_row, norm_w):
    bsz, s, _ = mqk.shape
    L = MLSTM_CHUNK
    tok = lambda width: pl.BlockSpec((None, L, width), lambda b, c: (b, c, 0))
    full = lambda shape: pl.BlockSpec(shape, lambda b, c: (0, 0))
    return pl.pallas_call(
        _mlstm_kernel,
        grid=(bsz, s // L),
        in_specs=[tok(MQK_COLS), tok(MLSTM_W), tok(MLSTM_W), tok(IF_COLS),
                  full((CONV_WIDTH, MQK_COLS)), full((1, IF_COLS)), full((1, MLSTM_W))],
        out_specs=tok(MLSTM_W),
        out_shape=jax.ShapeDtypeStruct((bsz, s, MLSTM_W), BF16),
        scratch_shapes=[pltpu.VMEM((L + SUBLANES, MQK_COLS), F32),
                        pltpu.VMEM((MLSTM_HEADS, MLSTM_HEAD_DIM, MLSTM_HEAD_DIM), F32),
                        pltpu.VMEM((SUBLANES, MLSTM_HEAD_DIM), F32),
                        pltpu.VMEM((SUBLANES, LANES), F32)],
        compiler_params=_cparams(("arbitrary", "arbitrary")),
        name="mlstm",
    )(mqk, mv, mo, gif, conv_w, bias_row, norm_w.reshape(1, MLSTM_W))


def _merge_kernel(x_ref, att_ref, mls_ref, g_ref, gate_ref, wa_ref, wm_ref, wo_ref, h_ref):
    a = jnp.dot(att_ref[...], wa_ref[...], preferred_element_type=F32)
    m = jnp.dot(mls_ref[...], wm_ref[...], preferred_element_type=F32)
    ga = g_ref[:, :D_MODEL].astype(F32)
    gm = g_ref[:, D_MODEL:].astype(F32)
    merged = jax.nn.sigmoid(ga) * a + jax.nn.sigmoid(gm) * m
    y = jnp.dot(merged.astype(BF16), wo_ref[...], preferred_element_type=F32)
    h_ref[...] = x_ref[...] + gate_ref[...] * y


def _merge(x, att, mls, g, gate1, w_att, w_mls, w_out):
    bsz, s, d = x.shape
    tm = TOK_TILE
    tok = lambda width: pl.BlockSpec((None, tm, width), lambda b, i: (b, i, 0))
    full = lambda shape: pl.BlockSpec(shape, lambda b, i: (0, 0))
    return pl.pallas_call(
        _merge_kernel,
        grid=(bsz, s // tm),
        in_specs=[tok(d), tok(ATT_Q_W), tok(MLSTM_W), tok(GATE_COLS),
                  pl.BlockSpec((None, 1, d), lambda b, i: (b, 0, 0)),
                  full((ATT_Q_W, d)), full((MLSTM_W, d)), full((d, d))],
        out_specs=tok(d),
        out_shape=jax.ShapeDtypeStruct((bsz, s, d), F32),
        compiler_params=_cparams(("parallel", "parallel")),
        name="merge",
    )(x, att, mls, g, gate1, w_att, w_mls, w_out)


def _top16(s, iota_f):
    n_keys = s.shape[0]
    work = s
    rank = jnp.full(s.shape, float(PEER_TOPK), F32)
    vals = []
    for a in range(PEER_TOPK):
        m = jnp.max(work, axis=0, keepdims=True)
        idx = jnp.min(jnp.where(work == m, iota_f, float(n_keys)), axis=0, keepdims=True)
        sel = iota_f == idx
        rank = jnp.where(sel, float(a), rank)
        work = jnp.where(sel, NEG_INF, work)
        vals.append(m)
    return vals, rank


def _route_kernel(h_ref, nw_ref, sc_ref, sh_ref, wq_ref, keys_ref,
                  r2_ref, e2_ref, cnt_ref, c1_ref, qy_ref):
    k = PEER_TOPK
    nk = PEER_N_KEYS
    u2 = (_rmsnorm_rows(h_ref[...], nw_ref[...]) * (1.0 + sc_ref[...]) + sh_ref[...]).astype(BF16)
    qy_ref[...] = lax.dot_general(wq_ref[...], u2, (((1,), (1,)), ((), ())),
                                  preferred_element_type=F32)
    n_col = h_ref.shape[0] // LANES
    iota_keys = lax.broadcasted_iota(jnp.int32, (nk, LANES), 0).astype(F32)
    iota_k = lax.broadcasted_iota(jnp.int32, (k, LANES), 0).astype(F32)

    def body(it, carry):
        head = it // n_col
        tcol = it % n_col
        lanes = pl.ds(pl.multiple_of(tcol * LANES, LANES), LANES)
        rows0 = pl.ds(pl.multiple_of(head * (2 * PEER_HALF), PEER_HALF), PEER_HALF)
        rows1 = pl.ds(pl.multiple_of(head * (2 * PEER_HALF) + PEER_HALF, PEER_HALF), PEER_HALF)
        q0 = qy_ref[rows0, lanes].astype(BF16)
        q1 = qy_ref[rows1, lanes].astype(BF16)
        s1 = jnp.dot(keys_ref[0], q0, preferred_element_type=F32)
        s2 = jnp.dot(keys_ref[1], q1, preferred_element_type=F32)
        vals0, rank0 = _top16(s1, iota_keys)
        vals1, rank1 = _top16(s2, iota_keys)
        v0 = jnp.concatenate(vals0, axis=0)
        v1 = jnp.concatenate(vals1, axis=0)

        cnt = jnp.zeros((k, LANES), F32)
        front = v0 + vals1[0]
        c_max = vals0[0] + vals1[0]
        z = jnp.zeros((1, LANES), F32)
        for _ in range(k):
            m = jnp.max(front, axis=0, keepdims=True)
            idx = jnp.min(jnp.where(front == m, iota_k, float(k)), axis=0, keepdims=True)
            sel = iota_k == idx
            cnt = cnt + jnp.where(sel, 1.0, 0.0)
            z = z + jnp.exp(m - c_max)
            taken = jnp.sum(jnp.where(sel, cnt, 0.0), axis=0, keepdims=True)
            nxt = jnp.full((1, LANES), NEG_INF, F32)
            for b in range(1, k):
                nxt = jnp.where(taken == float(b), vals1[b], nxt)
            front = jnp.where(sel, v0 + nxt, front)

        cnt_keys = jnp.zeros((nk, LANES), F32)
        for a in range(k):
            cnt_keys = jnp.where(rank0 == float(a), cnt[a:a + 1, :], cnt_keys)
        r2_ref[head, :, lanes] = rank1
        e2_ref[head, :, lanes] = jnp.exp(s2 - vals1[0])
        cnt_ref[head, :, lanes] = cnt_keys
        c1_ref[head, :, lanes] = jnp.exp(s1 - vals0[0]) / z
        return carry

    lax.fori_loop(0, PEER_HEADS * n_col, body, 0)


def _route(h, norm_w, sc, sh, wq_t, keys):
    bsz, s, d = h.shape
    tt = PEER_TOK
    sel = pl.BlockSpec((None, PEER_HEADS, PEER_N_KEYS, tt), lambda b, t: (b, 0, 0, t))
    out = jax.ShapeDtypeStruct((bsz, PEER_HEADS, PEER_N_KEYS, s), F32)
    qdim = wq_t.shape[0]
    return pl.pallas_call(
        _route_kernel,
        grid=(bsz, s // tt),
        in_specs=[pl.BlockSpec((None, tt, d), lambda b, t: (b, t, 0)),
                  pl.BlockSpec((1, d), lambda b, t: (0, 0)),
                  pl.BlockSpec((None, 1, d), lambda b, t: (b, 0, 0)),
                  pl.BlockSpec((None, 1, d), lambda b, t: (b, 0, 0)),
                  pl.BlockSpec((qdim, d), lambda b, t: (0, 0)),
                  pl.BlockSpec((2, PEER_N_KEYS, PEER_HALF), lambda b, t: (0, 0, 0))],
        out_specs=[sel, sel, sel, sel],
        out_shape=[out, out, out, out],
        scratch_shapes=[pltpu.VMEM((qdim, tt), F32)],
        compiler_params=_cparams(("parallel", "parallel")),
        name="route",
    )(h, norm_w.reshape(1, d), sc, sh, wq_t, keys)


def _peer_kernel(h_ref, nw_ref, sc_ref, sh_ref, gate_ref, nf_ref, u_ref, vt_ref,
                 r2_ref, e2_ref, cnt_ref, c1_ref, o_ref, x_ref, at_ref, wg_ref, acc_ref, *, final_norm):
    c = pl.program_id(2)
    tt = h_ref.shape[0]
    nk = PEER_N_KEYS

    @pl.when(c == 0)
    def _():
        u2 = _rmsnorm_rows(h_ref[...], nw_ref[...]) * (1.0 + sc_ref[...]) + sh_ref[...]
        x_ref[...] = u2.astype(BF16)
        acc_ref[...] = jnp.zeros_like(acc_ref)

    at_ref[...] = lax.dot_general(u_ref[...], x_ref[...], (((1,), (1,)), ((), ())),
                                  preferred_element_type=F32)

    def body(tcol, carry):
        lanes = pl.ds(pl.multiple_of(tcol * LANES, LANES), LANES)
        for ii in range(PEER_IBLK):
            rows = slice(ii * nk, (ii + 1) * nk)
            w = jnp.zeros((nk, LANES), F32)
            for hd in range(PEER_HEADS):
                sel = r2_ref[hd, :, lanes] < cnt_ref[hd, ii:ii + 1, lanes]
                w = w + jnp.where(sel, e2_ref[hd, :, lanes], 0.0) * c1_ref[hd, ii:ii + 1, lanes]
            a = at_ref[rows, lanes]
            act = 0.5 * a * (1.0 + lax.erf(a * np.float32(math.sqrt(0.5))))
            wg_ref[rows, lanes] = (w * act).astype(BF16)
        return carry

    lax.fori_loop(0, tt // LANES, body, 0)
    acc_ref[...] += jnp.dot(vt_ref[...], wg_ref[...], preferred_element_type=F32)

    @pl.when(c == pl.num_programs(2) - 1)
    def _():
        hh = h_ref[...] + gate_ref[...] * acc_ref[...].T
        o_ref[...] = _rmsnorm_rows(hh, nf_ref[...]) if final_norm else hh


def _peer(h, norm_w, sc, sh, gate2, normf_w, u_bf, vt_bf, r2, e2, cnt, c1, final_norm):
    bsz, s, d = h.shape
    tt = PEER_TOK
    ec = PEER_IBLK * PEER_N_KEYS
    n_exp = u_bf.shape[0]
    vec = pl.BlockSpec((1, d), lambda b, t, c: (0, 0))
    bvec = pl.BlockSpec((None, 1, d), lambda b, t, c: (b, 0, 0))
    selj = pl.BlockSpec((None, PEER_HEADS, PEER_N_KEYS, tt), lambda b, t, c: (b, 0, 0, t))
    seli = pl.BlockSpec((None, PEER_HEADS, PEER_IBLK, tt), lambda b, t, c: (b, 0, c, t))
    return pl.pallas_call(
        functools.partial(_peer_kernel, final_norm=final_norm),
        grid=(bsz, s // tt, n_exp // ec),
        in_specs=[pl.BlockSpec((None, tt, d), lambda b, t, c: (b, t, 0)),
                  vec, bvec, bvec, bvec, vec,
                  pl.BlockSpec((ec, d), lambda b, t, c: (c, 0)),
                  pl.BlockSpec((d, ec), lambda b, t, c: (0, c)),
                  selj, selj, seli, seli],
        out_specs=pl.BlockSpec((None, tt, d), lambda b, t, c: (b, t, 0)),
        out_shape=jax.ShapeDtypeStruct((bsz, s, d), F32),
        scratch_shapes=[pltpu.VMEM((tt, d), BF16),
                        pltpu.VMEM((ec, tt), F32),
                        pltpu.VMEM((ec, tt), BF16),
                        pltpu.VMEM((d, tt), F32)],
        compiler_params=_cparams(("parallel", "parallel", "arbitrary")),
        name="peer",
    )(h, norm_w.reshape(1, d), sc, sh, gate2, normf_w.reshape(1, d), u_bf, vt_bf, r2, e2, cnt, c1)


def _rope_tables(s):
    half = ATT_HEAD_DIM // 2
    inv_freq = ROPE_THETA ** (-jnp.arange(half, dtype=F32) * 2.0 / ATT_HEAD_DIM)
    ang = jnp.arange(s, dtype=F32)[:, None] * inv_freq[None, :]
    cos, sin = jnp.cos(ang), jnp.sin(ang)
    reps = LANES // ATT_HEAD_DIM
    cos_t = jnp.tile(jnp.concatenate([cos, cos], axis=-1), (1, reps))
    sin_t = jnp.tile(jnp.concatenate([-sin, sin], axis=-1), (1, reps))
    return cos_t, sin_t


def _permute_in_proj(w_in):
    n_if = 2 * MLSTM_HEADS
    lo = ATT_COLS + MQK_COLS + 2 * MLSTM_W
    pad = jnp.zeros((w_in.shape[0], IF_COLS - n_if), w_in.dtype)
    return jnp.concatenate([w_in[:, :lo], w_in[:, lo + n_if:], w_in[:, lo:lo + n_if], pad], axis=1)


def kernel(x, c, w_ada, b_ada, norm1_w, w_in, conv_w, att_sinks, i_bias, f_bias, mlstm_norm_w,
           w_att_branch, w_mlstm_branch, w_out, norm2_w, peer_w_query, peer_sub_keys, peer_u,
           peer_v, norm_f_w):
    depth = w_ada.shape[0]
    bsz, s, d = x.shape
    cos_t, sin_t = _rope_tables(s)
    h = x
    for l in range(depth):
        mod = _ada(c, w_ada[l], b_ada[l]).reshape(bsz, 6, 1, d)
        sh1, sc1, g1, sh2, sc2, g2 = (mod[:, i] for i in range(6))

        w_perm = _permute_in_proj(w_in[l]).astype(BF16)
        att_in, mqk, mv, mo, gates, gif = _inproj(h, norm1_w[l], sc1, sh1, w_perm)
        att = _attention(att_in, att_sinks[l], cos_t, sin_t)
        bias_row = jnp.concatenate(
            [i_bias[l], f_bias[l], jnp.zeros((IF_COLS - 2 * MLSTM_HEADS,), F32)]).reshape(1, IF_COLS)
        mls = _mlstm(mqk, mv, mo, gif, conv_w[l], bias_row, mlstm_norm_w[l])
        h = _merge(h, att, mls, gates, g1, w_att_branch[l].astype(BF16),
                   w_mlstm_branch[l].astype(BF16), w_out[l].astype(BF16))

        wq_t = peer_w_query[l].T.astype(BF16)
        keys = peer_sub_keys[l].astype(BF16)
        r2, e2, cnt, c1 = _route(h, norm2_w[l], sc2, sh2, wq_t, keys)
        h = _peer(h, norm2_w[l], sc2, sh2, g2, norm_f_w, peer_u[l].astype(BF16),
                  peer_v[l].T.astype(BF16), r2, e2, cnt, c1, final_norm=(l == depth - 1))
    return h
```

```python
import functools
import math

import numpy as np
import jax
import jax.numpy as jnp
from jax import lax
from jax.experimental import pallas as pl
from jax.experimental.pallas import tpu as pltpu

F32 = jnp.float32
BF16 = jnp.bfloat16
HIGHEST = lax.Precision.HIGHEST

D_MODEL = 1024
ATT_HEADS = 8
ATT_KV_HEADS = 2
ATT_HEAD_DIM = 64
ATT_BLOCK = 128
ROPE_THETA = 10000.0
MLSTM_HEADS = 4
MLSTM_HEAD_DIM = 128
MLSTM_CHUNK = 128
CONV_WIDTH = 4
ATT_Q_W = ATT_HEADS * ATT_HEAD_DIM
ATT_KV_W = ATT_KV_HEADS * ATT_HEAD_DIM
MLSTM_W = MLSTM_HEADS * MLSTM_HEAD_DIM
PEER_HEADS = 8
PEER_N_KEYS = 128
PEER_HALF = 128
PEER_TOPK = 16
EPS = 1e-6

LANES = 128
SUBLANES = 8
BF16_ROWS = 16
VMEM_LIMIT = 48 * 1024 * 1024

ATT_COLS = ATT_Q_W + 2 * ATT_KV_W
MQK_COLS = 2 * MLSTM_W
GATE_COLS = 2 * D_MODEL
IF_COLS = LANES
OFF_ATT = 0
OFF_MQK = OFF_ATT + ATT_COLS
OFF_MV = OFF_MQK + MQK_COLS
OFF_MO = OFF_MV + MLSTM_W
OFF_G = OFF_MO + MLSTM_W
OFF_IF = OFF_G + GATE_COLS
IN_COLS = OFF_IF + IF_COLS

TOK_TILE = 512
PEER_TOK = 512
PEER_IBLK = 8
PEER_GROUP_KEYS = 2
NEG_INF = float("-inf")


def _cparams(sem):
    return pltpu.CompilerParams(dimension_semantics=sem, vmem_limit_bytes=VMEM_LIMIT)


def _rmsnorm_rows(x, w):
    return x * lax.rsqrt(jnp.mean(x * x, axis=-1, keepdims=True) + EPS) * w


def _ada_kernel(c_ref, w_ref, b_ref, o_ref):
    c = c_ref[...]
    act = c * jax.nn.sigmoid(c)
    o_ref[...] = jnp.dot(act, w_ref[...], preferred_element_type=F32, precision=HIGHEST) + b_ref[...]


def _ada(c, w, b):
    bsz, d = c.shape
    n_out = w.shape[1]
    return pl.pallas_call(
        _ada_kernel,
        grid=(n_out // d,),
        in_specs=[pl.BlockSpec((bsz, d), lambda j: (0, 0)),
                  pl.BlockSpec((d, d), lambda j: (0, j)),
                  pl.BlockSpec((1, d), lambda j: (0, j))],
        out_specs=pl.BlockSpec((bsz, d), lambda j: (0, j)),
        out_shape=jax.ShapeDtypeStruct((bsz, n_out), F32),
        compiler_params=_cparams(("arbitrary",)),
        name="ada",
    )(c, w, b.reshape(1, n_out))


def _inproj_kernel(x_ref, nw_ref, sc_ref, sh_ref, w_ref,
                   att_ref, mqk_ref, mv_ref, mo_ref, g_ref, if_ref):
    y = _rmsnorm_rows(x_ref[...], nw_ref[...])
    u = (y * (1.0 + sc_ref[...]) + sh_ref[...]).astype(BF16)

    def proj(lo, width):
        return jnp.dot(u, w_ref[:, lo:lo + width], preferred_element_type=F32)

    att_ref[...] = proj(OFF_ATT, ATT_COLS).astype(BF16)
    mqk_ref[...] = proj(OFF_MQK, MQK_COLS).astype(BF16)
    mv_ref[...] = proj(OFF_MV, MLSTM_W).astype(BF16)
    mo_ref[...] = proj(OFF_MO, MLSTM_W).astype(BF16)
    g_ref[...] = proj(OFF_G, GATE_COLS).astype(BF16)
    if_ref[...] = proj(OFF_IF, IF_COLS)


def _inproj(x, norm_w, sc, sh, w_perm):
    bsz, s, d = x.shape
    tm = TOK_TILE
    tok = lambda width: pl.BlockSpec((None, tm, width), lambda b, i: (b, i, 0))
    vec = pl.BlockSpec((None, 1, d), lambda b, i: (b, 0, 0))
    widths = (ATT_COLS, MQK_COLS, MLSTM_W, MLSTM_W, GATE_COLS, IF_COLS)
    dtypes = (BF16, BF16, BF16, BF16, BF16, F32)
    return pl.pallas_call(
        _inproj_kernel,
        grid=(bsz, s // tm),
        in_specs=[tok(d),
                  pl.BlockSpec((1, d), lambda b, i: (0, 0)),
                  vec, vec,
                  pl.BlockSpec((d, IN_COLS), lambda b, i: (0, 0))],
        out_specs=[tok(w) for w in widths],
        out_shape=[jax.ShapeDtypeStruct((bsz, s, w), dt) for w, dt in zip(widths, dtypes)],
        compiler_params=_cparams(("parallel", "parallel")),
        name="inproj",
    )(x, norm_w.reshape(1, d), sc, sh, w_perm)


def _attn_kernel(sink_ref, cur_ref, prev_ref, cosc_ref, sinc_ref, cosp_ref, sinp_ref, o_ref):
    blk = ATT_BLOCK
    n = pl.program_id(1)
    lane = lax.broadcasted_iota(jnp.int32, (blk, LANES), 1)
    first_half = (lane & (ATT_HEAD_DIM - 1)) < (ATT_HEAD_DIM // 2)
    low = lane < ATT_HEAD_DIM

    def rope(xf, cos, sin):
        rot = jnp.where(first_half, pltpu.roll(xf, LANES - ATT_HEAD_DIM // 2, 1),
                        pltpu.roll(xf, ATT_HEAD_DIM // 2, 1))
        return xf * cos + rot * sin

    def dup(xf, g):
        sw = pltpu.roll(xf, ATT_HEAD_DIM, 1)
        return jnp.where(low, xf, sw) if g == 0 else jnp.where(low, sw, xf)

    cosc, sinc = cosc_ref[...], sinc_ref[...]
    cosp, sinp = cosp_ref[...], sinp_ref[...]
    k_cur = rope(cur_ref[:, ATT_Q_W:ATT_Q_W + ATT_KV_W].astype(F32), cosc, sinc)
    k_prev = rope(prev_ref[:, ATT_Q_W:ATT_Q_W + ATT_KV_W].astype(F32), cosp, sinp)
    v_cur = cur_ref[:, ATT_Q_W + ATT_KV_W:ATT_COLS].astype(F32)
    v_prev = prev_ref[:, ATT_Q_W + ATT_KV_W:ATT_COLS].astype(F32)

    qi = lax.broadcasted_iota(jnp.int32, (blk, 2 * blk), 0)
    ki = lax.broadcasted_iota(jnp.int32, (blk, 2 * blk), 1)
    q_lim = jnp.where(n > 0, qi, 2 * blk)
    mask = ((ki < blk) & (ki > q_lim)) | ((ki >= blk) & ((ki - blk) <= qi))

    scale = ATT_HEAD_DIM ** -0.5
    group = ATT_HEADS // ATT_KV_HEADS
    for g in range(ATT_KV_HEADS):
        kk = jnp.concatenate([dup(k_prev, g), dup(k_cur, g)], axis=0).astype(BF16)
        vv = jnp.concatenate([dup(v_prev, g), dup(v_cur, g)], axis=0).astype(BF16)
        for pair in range(group // 2):
            j = g * (group // 2) + pair
            qp = rope(cur_ref[:, j * LANES:(j + 1) * LANES].astype(F32), cosc, sinc) * scale
            outs = []
            for half in range(2):
                head = 2 * j + half
                qh = jnp.where(low, qp, 0.0) if half == 0 else jnp.where(low, 0.0, qp)
                s = lax.dot_general(qh.astype(BF16), kk, (((1,), (1,)), ((), ())),
                                    preferred_element_type=F32)
                s = jnp.where(mask, s, NEG_INF)
                sink = sink_ref[head]
                m = jnp.maximum(jnp.max(s, axis=-1, keepdims=True), sink)
                p = jnp.exp(s - m)
                denom = jnp.sum(p, axis=-1, keepdims=True) + jnp.exp(sink - m)
                o = jnp.dot(p.astype(BF16), vv, preferred_element_type=F32)
                outs.append(o / denom)
            o_ref[:, j * LANES:(j + 1) * LANES] = jnp.where(low, outs[0], outs[1]).astype(BF16)


def _attention(att_in, sinks, cos_t, sin_t):
    bsz, s, _ = att_in.shape
    blk = ATT_BLOCK
    nb = s // blk
    prev = lambda n: jnp.maximum(n - 1, 0)
    return pl.pallas_call(
        _attn_kernel,
        grid=(bsz, nb),
        in_specs=[pl.BlockSpec(memory_space=pltpu.SMEM),
                  pl.BlockSpec((None, blk, ATT_COLS), lambda b, n: (b, n, 0)),
                  pl.BlockSpec((None, blk, ATT_COLS), lambda b, n: (b, prev(n), 0)),
                  pl.BlockSpec((blk, LANES), lambda b, n: (n, 0)),
                  pl.BlockSpec((blk, LANES), lambda b, n: (n, 0)),
                  pl.BlockSpec((blk, LANES), lambda b, n: (prev(n), 0)),
                  pl.BlockSpec((blk, LANES), lambda b, n: (prev(n), 0))],
        out_specs=pl.BlockSpec((None, blk, ATT_Q_W), lambda b, n: (b, n, 0)),
        out_shape=jax.ShapeDtypeStruct((bsz, s, ATT_Q_W), BF16),
        compiler_params=_cparams(("parallel", "parallel")),
        name="attn",
    )(sinks, att_in, att_in, cos_t, sin_t, cos_t, sin_t)


def _mlstm_kernel(mqk_ref, mv_ref, mo_ref, if_ref, convw_ref, bias_ref, nw_ref,
                  o_ref, buf_ref, c_ref, n_ref, m_ref):
    L = MLSTM_CHUNK
    d = MLSTM_HEAD_DIM
    tail = SUBLANES

    @pl.when(pl.program_id(1) == 0)
    def _():
        buf_ref[0:tail, :] = jnp.zeros((tail, 2 * MLSTM_W), F32)
        c_ref[...] = jnp.zeros_like(c_ref)
        n_ref[...] = jnp.zeros_like(n_ref)
        m_ref[...] = jnp.zeros_like(m_ref)

    buf_ref[tail:tail + L, :] = mqk_ref[...].astype(F32)
    y = jnp.zeros((L, 2 * MLSTM_W), F32)
    for j in range(CONV_WIDTH):
        off = tail - (CONV_WIDTH - 1) + j
        y = y + convw_ref[j:j + 1, :] * buf_ref[off:off + L, :]
    buf_ref[0:tail, :] = buf_ref[L:L + tail, :]
    y = y * jax.nn.sigmoid(y)
    q_all = y[:, :MLSTM_W]
    k_all = y[:, MLSTM_W:] * (d ** -0.5)

    gates = if_ref[...] + bias_ref[...]
    lf = jnp.minimum(gates, 0.0) - jnp.log1p(jnp.exp(-jnp.abs(gates)))
    row = lax.broadcasted_iota(jnp.int32, (L, L), 0)
    col = lax.broadcasted_iota(jnp.int32, (L, L), 1)
    causal = col <= row
    tril = causal.astype(F32)
    triu = (row <= col).astype(F32)
    bcum_c = jnp.dot(tril, lf, preferred_element_type=F32, precision=HIGHEST)
    gates_t = gates.T
    bcum_r = jnp.dot(lf.T, triu, preferred_element_type=F32, precision=HIGHEST)

    for h in range(MLSTM_HEADS):
        hs = slice(h * d, (h + 1) * d)
        fl = MLSTM_HEADS + h
        q = q_all[:, hs]
        k = k_all[:, hs]
        v = mv_ref[:, hs]
        ic_c = gates[:, h:h + 1]
        ic_r = gates_t[h:h + 1, :]
        b_c = bcum_c[:, fl:fl + 1]
        b_r = bcum_r[fl:fl + 1, :]
        g_tot = b_c[L - 1:L, :]
        c_prev = c_ref[h]
        n_prev = n_ref[h:h + 1, :]
        m_prev = m_ref[h:h + 1, 0:1]

        dmat = jnp.where(causal, b_c - b_r + ic_r, NEG_INF)
        m_inter = b_c + m_prev
        m_out = jnp.maximum(m_inter, jnp.max(dmat, axis=-1, keepdims=True))
        qb = q.astype(BF16)
        kb = k.astype(BF16)
        qk = lax.dot_general(qb, kb, (((1,), (1,)), ((), ())), preferred_element_type=F32)
        w_intra = jnp.exp(dmat - m_out) * qk
        w_inter = jnp.exp(m_inter - m_out)
        num = (jnp.dot(w_intra.astype(BF16), v, preferred_element_type=F32)
               + w_inter * jnp.dot(qb, c_prev.astype(BF16), preferred_element_type=F32))
        den = (jnp.sum(w_intra, axis=-1, keepdims=True)
               + w_inter * jnp.sum(q * n_prev, axis=-1, keepdims=True))
        hid = num / jnp.maximum(jnp.abs(den), jnp.exp(-m_out))
        hid = jax.nn.sigmoid(mo_ref[:, hs].astype(F32)) * hid
        hid = hid * lax.rsqrt(jnp.mean(hid * hid, axis=-1, keepdims=True) + EPS)
        o_ref[:, hs] = (hid * nw_ref[:, hs]).astype(BF16)

        a_c = g_tot - b_c + ic_c
        m_new = jnp.maximum(g_tot + m_prev, jnp.max(a_c, axis=0, keepdims=True))
        decay = jnp.exp(g_tot + m_prev - m_new)
        wk = jnp.exp(a_c - m_new) * k
        c_ref[h] = decay * c_prev + lax.dot_general(
            wk.astype(BF16), v, (((0,), (0,)), ((), ())), preferred_element_type=F32)
        n_ref[h:h + 1, :] = decay * n_prev + jnp.sum(wk, axis=0, keepdims=True)
        m_ref[h:h + 1, :] = jnp.broadcast_to(m_new, (1, LANES))


def _mlstm(mqk, mv, mo, gif, conv_w, bias_row, norm_w):
    bsz, s, _ = mqk.shape
    L = MLSTM_CHUNK
    tok = lambda width: pl.BlockSpec((None, L, width), lambda b, c: (b, c, 0))
    full = lambda shape: pl.BlockSpec(shape, lambda b, c: (0, 0))
    return pl.pallas_call(
        _mlstm_kernel,
        grid=(bsz, s // L),
        in_specs=[tok(MQK_COLS), tok(MLSTM_W), tok(MLSTM_W), tok(IF_COLS),
                  full((CONV_WIDTH, MQK_COLS)), full((1, IF_COLS)), full((1, MLSTM_W))],
        out_specs=tok(MLSTM_W),
        out_shape=jax.ShapeDtypeStruct((bsz, s, MLSTM_W), BF16),
        scratch_shapes=[pltpu.VMEM((L + SUBLANES, MQK_COLS), F32),
                        pltpu.VMEM((MLSTM_HEADS, MLSTM_HEAD_DIM, MLSTM_HEAD_DIM), F32),
                        pltpu.VMEM((SUBLANES, MLSTM_HEAD_DIM), F32),
                        pltpu.VMEM((SUBLANES, LANES), F32)],
        compiler_params=_cparams(("arbitrary", "arbitrary")),
        name="mlstm",
    )(mqk, mv, mo, gif, conv_w, bias_row, norm_w.reshape(1, MLSTM_W))


def _merge_kernel(x_ref, att_ref, mls_ref, g_ref, gate_ref, wa_ref, wm_ref, wo_ref, h_ref):
    a = jnp.dot(att_ref[...], wa_ref[...], preferred_element_type=F32)
    m = jnp.dot(mls_ref[...], wm_ref[...], preferred_element_type=F32)
    ga = g_ref[:, :D_MODEL].astype(F32)
    gm = g_ref[:, D_MODEL:].astype(F32)
    merged = jax.nn.sigmoid(ga) * a + jax.nn.sigmoid(gm) * m
    y = jnp.dot(merged.astype(BF16), wo_ref[...], preferred_element_type=F32)
    h_ref[...] = x_ref[...] + gate_ref[...] * y


def _merge(x, att, mls, g, gate1, w_att, w_mls, w_out):
    bsz, s, d = x.shape
    tm = TOK_TILE
    tok = lambda width: pl.BlockSpec((None, tm, width), lambda b, i: (b, i, 0))
    full = lambda shape: pl.BlockSpec(shape, lambda b, i: (0, 0))
    return pl.pallas_call(
        _merge_kernel,
        grid=(bsz, s // tm),
        in_specs=[tok(d), tok(ATT_Q_W), tok(MLSTM_W), tok(GATE_COLS),
                  pl.BlockSpec((None, 1, d), lambda b, i: (b, 0, 0)),
                  full((ATT_Q_W, d)), full((MLSTM_W, d)), full((d, d))],
        out_specs=tok(d),
        out_shape=jax.ShapeDtypeStruct((bsz, s, d), F32),
        compiler_params=_cparams(("parallel", "parallel")),
        name="merge",
    )(x, att, mls, g, gate1, w_att, w_mls, w_out)


def _top16(s, iota_f):
    n_keys = s.shape[0]
    work = s
    rank = jnp.full(s.shape, float(PEER_TOPK), F32)
    vals = []
    for a in range(PEER_TOPK):
        m = jnp.max(work, axis=0, keepdims=True)
        idx = jnp.min(jnp.where(work == m, iota_f, float(n_keys)), axis=0, keepdims=True)
        sel = iota_f == idx
        rank = jnp.where(sel, float(a), rank)
        work = jnp.where(sel, NEG_INF, work)
        vals.append(m)
    return vals, rank


def _route_kernel(h_ref, nw_ref, sc_ref, sh_ref, wq_ref, keys_ref,
                  r2_ref, e2_ref, cnt_ref, c1_ref, qy_ref):
    k = PEER_TOPK
    nk = PEER_N_KEYS
    u2 = (_rmsnorm_rows(h_ref[...], nw_ref[...]) * (1.0 + sc_ref[...]) + sh_ref[...]).astype(BF16)
    qy_ref[...] = lax.dot_general(wq_ref[...], u2, (((1,), (1,)), ((), ())),
                                  preferred_element_type=F32)
    n_col = h_ref.shape[0] // LANES
    iota_keys = lax.broadcasted_iota(jnp.int32, (nk, LANES), 0).astype(F32)
    iota_k = lax.broadcasted_iota(jnp.int32, (k, LANES), 0).astype(F32)

    def body(it, carry):
        head = it // n_col
        tcol = it % n_col
        lanes = pl.ds(pl.multiple_of(tcol * LANES, LANES), LANES)
        rows0 = pl.ds(pl.multiple_of(head * (2 * PEER_HALF), PEER_HALF), PEER_HALF)
        rows1 = pl.ds(pl.multiple_of(head * (2 * PEER_HALF) + PEER_HALF, PEER_HALF), PEER_HALF)
        q0 = qy_ref[rows0, lanes].astype(BF16)
        q1 = qy_ref[rows1, lanes].astype(BF16)
        s1 = jnp.dot(keys_ref[0], q0, preferred_element_type=F32)
        s2 = jnp.dot(keys_ref[1], q1, preferred_element_type=F32)
        vals0, rank0 = _top16(s1, iota_keys)
        vals1, rank1 = _top16(s2, iota_keys)
        v0 = jnp.concatenate(vals0, axis=0)
        v1 = jnp.concatenate(vals1, axis=0)

        cnt = jnp.zeros((k, LANES), F32)
        front = v0 + vals1[0]
        c_max = vals0[0] + vals1[0]
        z = jnp.zeros((1, LANES), F32)
        for _ in range(k):
            m = jnp.max(front, axis=0, keepdims=True)
            idx = jnp.min(jnp.where(front == m, iota_k, float(k)), axis=0, keepdims=True)
            sel = iota_k == idx
            cnt = cnt + jnp.where(sel, 1.0, 0.0)
            z = z + jnp.exp(m - c_max)
            taken = jnp.sum(jnp.where(sel, cnt, 0.0), axis=0, keepdims=True)
            nxt = jnp.full((1, LANES), NEG_INF, F32)
            for b in range(1, k):
                nxt = jnp.where(taken == float(b), vals1[b], nxt)
            front = jnp.where(sel, v0 + nxt, front)

        cnt_keys = jnp.zeros((nk, LANES), F32)
        for a in range(k):
            cnt_keys = jnp.where(rank0 == float(a), cnt[a:a + 1, :], cnt_keys)
        r2_ref[head, tcol] = pltpu.bitcast(rank1.astype(BF16), jnp.int32)
        e2_ref[head, tcol] = pltpu.bitcast(jnp.exp(s2 - vals1[0]).astype(BF16), jnp.int32)
        cnt_ref[head, tcol] = cnt_keys
        c1_ref[head, tcol] = jnp.exp(s1 - vals0[0]) / z
        return carry

    lax.fori_loop(0, PEER_HEADS * n_col, body, 0)


def _route(h, norm_w, sc, sh, wq_t, keys):
    bsz, s, d = h.shape
    tt = PEER_TOK
    sel = lambda rows: pl.BlockSpec((None, PEER_HEADS, tt // LANES, rows, LANES), lambda b, t: (b, 0, t, 0, 0))
    out = lambda rows, dt: jax.ShapeDtypeStruct((bsz, PEER_HEADS, s // LANES, rows, LANES), dt)
    packed = PEER_N_KEYS * 2 // 4
    qdim = wq_t.shape[0]
    return pl.pallas_call(
        _route_kernel,
        grid=(bsz, s // tt),
        in_specs=[pl.BlockSpec((None, tt, d), lambda b, t: (b, t, 0)),
                  pl.BlockSpec((1, d), lambda b, t: (0, 0)),
                  pl.BlockSpec((None, 1, d), lambda b, t: (b, 0, 0)),
                  pl.BlockSpec((None, 1, d), lambda b, t: (b, 0, 0)),
                  pl.BlockSpec((qdim, d), lambda b, t: (0, 0)),
                  pl.BlockSpec((2, PEER_N_KEYS, PEER_HALF), lambda b, t: (0, 0, 0))],
        out_specs=[sel(packed), sel(packed), sel(PEER_N_KEYS), sel(PEER_N_KEYS)],
        out_shape=[out(packed, jnp.int32), out(packed, jnp.int32),
                   out(PEER_N_KEYS, F32), out(PEER_N_KEYS, F32)],
        scratch_shapes=[pltpu.VMEM((qdim, tt), F32)],
        compiler_params=_cparams(("parallel", "parallel")),
        name="route",
    )(h, norm_w.reshape(1, d), sc, sh, wq_t, keys)


def _peer_kernel(h_ref, nw_ref, sc_ref, sh_ref, gate_ref, nf_ref, u_ref, vt_ref,
                 r2_ref, e2_ref, cnt_ref, c1_ref, o_ref,
                 x_ref, at_ref, wg_ref, acc_ref, *, final_norm):
    c = pl.program_id(2)
    tt = h_ref.shape[0]
    nk = PEER_N_KEYS
    ncol = tt // LANES
    gk = PEER_GROUP_KEYS

    @pl.when(c == 0)
    def _():
        u2 = _rmsnorm_rows(h_ref[...], nw_ref[...]) * (1.0 + sc_ref[...]) + sh_ref[...]
        x_ref[...] = u2.astype(BF16)
        acc_ref[...] = jnp.zeros_like(acc_ref)

    pre = lax.dot_general(u_ref[...], x_ref[...], (((1,), (1,)), ((), ())),
                          preferred_element_type=F32)
    for tcol in range(ncol):
        at_ref[tcol] = pre[:, tcol * LANES:(tcol + 1) * LANES]

    def row_tile(ref, hd, tcol, key):
        return jnp.broadcast_to(ref[hd, tcol, key:key + 1, :], (nk, LANES)).astype(BF16)

    def gate_column(tcol, carry):
        for k0 in range(0, PEER_IBLK, gk):
            w = [jnp.zeros((nk, LANES), BF16) for _ in range(gk)]
            for hd in range(PEER_HEADS):
                r2 = pltpu.bitcast(r2_ref[hd, tcol], BF16)
                e2 = pltpu.bitcast(e2_ref[hd, tcol], BF16)
                for k in range(gk):
                    sel = r2 < row_tile(cnt_ref, hd, tcol, k0 + k)
                    w[k] = w[k] + (jnp.where(sel, e2, jnp.zeros_like(e2))
                                   * row_tile(c1_ref, hd, tcol, k0 + k))
            for k in range(gk):
                rows = slice((k0 + k) * nk, (k0 + k + 1) * nk)
                a = at_ref[tcol, rows, :]
                act = 0.5 * a * (1.0 + lax.erf(a * np.float32(math.sqrt(0.5))))
                wg_ref[tcol, rows, :] = w[k] * act.astype(BF16)
        return carry

    lax.fori_loop(0, ncol, gate_column, 0)
    wg = jnp.concatenate([wg_ref[tcol] for tcol in range(ncol)], axis=1)
    acc_ref[...] += jnp.dot(vt_ref[...], wg, preferred_element_type=F32)

    @pl.when(c == pl.num_programs(2) - 1)
    def _():
        hh = h_ref[...] + gate_ref[...] * acc_ref[...].T
        o_ref[...] = _rmsnorm_rows(hh, nf_ref[...]) if final_norm else hh


def _peer(h, norm_w, sc, sh, gate2, normf_w, u_bf, vt_bf, r2, e2, cnt, c1, final_norm):
    bsz, s, d = h.shape
    tt = PEER_TOK
    ec = PEER_IBLK * PEER_N_KEYS
    n_exp = u_bf.shape[0]
    vec = pl.BlockSpec((1, d), lambda b, t, c: (0, 0))
    bvec = pl.BlockSpec((None, 1, d), lambda b, t, c: (b, 0, 0))
    ncol = tt // LANES
    selj = pl.BlockSpec((None, PEER_HEADS, ncol, PEER_N_KEYS // 2, LANES), lambda b, t, c: (b, 0, t, 0, 0))
    seli = pl.BlockSpec((None, PEER_HEADS, ncol, PEER_IBLK, LANES), lambda b, t, c: (b, 0, t, c, 0))
    return pl.pallas_call(
        functools.partial(_peer_kernel, final_norm=final_norm),
        grid=(bsz, s // tt, n_exp // ec),
        in_specs=[pl.BlockSpec((None, tt, d), lambda b, t, c: (b, t, 0)),
                  vec, bvec, bvec, bvec, vec,
                  pl.BlockSpec((ec, d), lambda b, t, c: (c, 0)),
                  pl.BlockSpec((d, ec), lambda b, t, c: (0, c)),
                  selj, selj, seli, seli],
        out_specs=pl.BlockSpec((None, tt, d), lambda b, t, c: (b, t, 0)),
        out_shape=jax.ShapeDtypeStruct((bsz, s, d), F32),
        scratch_shapes=[pltpu.VMEM((tt, d), BF16),
                        pltpu.VMEM((ncol, ec, LANES), F32),
                        pltpu.VMEM((ncol, ec, LANES), BF16),
                        pltpu.VMEM((d, tt), F32)],
        compiler_params=_cparams(("parallel", "parallel", "arbitrary")),
        name="peer",
    )(h, norm_w.reshape(1, d), sc, sh, gate2, normf_w.reshape(1, d), u_bf, vt_bf, r2, e2, cnt, c1)


def _rope_tables(s):
    half = ATT_HEAD_DIM // 2
    inv_freq = ROPE_THETA ** (-jnp.arange(half, dtype=F32) * 2.0 / ATT_HEAD_DIM)
    ang = jnp.arange(s, dtype=F32)[:, None] * inv_freq[None, :]
    cos, sin = jnp.cos(ang), jnp.sin(ang)
    reps = LANES // ATT_HEAD_DIM
    cos_t = jnp.tile(jnp.concatenate([cos, cos], axis=-1), (1, reps))
    sin_t = jnp.tile(jnp.concatenate([-sin, sin], axis=-1), (1, reps))
    return cos_t, sin_t


def _permute_in_proj(w_in):
    n_if = 2 * MLSTM_HEADS
    lo = ATT_COLS + MQK_COLS + 2 * MLSTM_W
    pad = jnp.zeros((w_in.shape[0], IF_COLS - n_if), w_in.dtype)
    return jnp.concatenate([w_in[:, :lo], w_in[:, lo + n_if:], w_in[:, lo:lo + n_if], pad], axis=1)


def kernel(x, c, w_ada, b_ada, norm1_w, w_in, conv_w, att_sinks, i_bias, f_bias, mlstm_norm_w,
           w_att_branch, w_mlstm_branch, w_out, norm2_w, peer_w_query, peer_sub_keys, peer_u,
           peer_v, norm_f_w):
    depth = w_ada.shape[0]
    bsz, s, d = x.shape
    cos_t, sin_t = _rope_tables(s)
    h = x
    for l in range(depth):
        mod = _ada(c, w_ada[l], b_ada[l]).reshape(bsz, 6, 1, d)
        sh1, sc1, g1, sh2, sc2, g2 = (mod[:, i] for i in range(6))

        w_perm = _permute_in_proj(w_in[l]).astype(BF16)
        att_in, mqk, mv, mo, gates, gif = _inproj(h, norm1_w[l], sc1, sh1, w_perm)
        att = _attention(att_in, att_sinks[l], cos_t, sin_t)
        bias_row = jnp.concatenate(
            [i_bias[l], f_bias[l], jnp.zeros((IF_COLS - 2 * MLSTM_HEADS,), F32)]).reshape(1, IF_COLS)
        mls = _mlstm(mqk, mv, mo, gif, conv_w[l], bias_row, mlstm_norm_w[l])
        h = _merge(h, att, mls, gates, g1, w_att_branch[l].astype(BF16),
                   w_mlstm_branch[l].astype(BF16), w_out[l].astype(BF16))

        wq_t = peer_w_query[l].T.astype(BF16)
        keys = peer_sub_keys[l].astype(BF16)
        r2, e2, cnt, c1 = _route(h, norm2_w[l], sc2, sh2, wq_t, keys)
        h = _peer(h, norm2_w[l], sc2, sh2, g2, norm_f_w, peer_u[l].astype(BF16),
                  peer_v[l].T.astype(BF16), r2, e2, cnt, c1, final_norm=(l == depth - 1))
    return h
```

```python
import functools
import math

import numpy as np
import jax
import jax.numpy as jnp
from jax import lax
from jax.experimental import pallas as pl
from jax.experimental.pallas import tpu as pltpu

F32 = jnp.float32
BF16 = jnp.bfloat16
HIGHEST = lax.Precision.HIGHEST

D_MODEL = 1024
ATT_HEADS = 8
ATT_KV_HEADS = 2
ATT_HEAD_DIM = 64
ATT_BLOCK = 128
ROPE_THETA = 10000.0
MLSTM_HEADS = 4
MLSTM_HEAD_DIM = 128
MLSTM_CHUNK = 128
CONV_WIDTH = 4
ATT_Q_W = ATT_HEADS * ATT_HEAD_DIM
ATT_KV_W = ATT_KV_HEADS * ATT_HEAD_DIM
MLSTM_W = MLSTM_HEADS * MLSTM_HEAD_DIM
PEER_HEADS = 8
PEER_N_KEYS = 128
PEER_HALF = 128
PEER_TOPK = 16
EPS = 1e-6

LANES = 128
SUBLANES = 8
BF16_ROWS = 16
VMEM_LIMIT = 48 * 1024 * 1024

ATT_COLS = ATT_Q_W + 2 * ATT_KV_W
MQK_COLS = 2 * MLSTM_W
GATE_COLS = 2 * D_MODEL
IF_COLS = LANES
OFF_ATT = 0
OFF_MQK = OFF_ATT + ATT_COLS
OFF_MV = OFF_MQK + MQK_COLS
OFF_MO = OFF_MV + MLSTM_W
OFF_G = OFF_MO + MLSTM_W
OFF_IF = OFF_G + GATE_COLS
IN_COLS = OFF_IF + IF_COLS

TOK_TILE = 512
PEER_TOK = 512
PEER_IBLK = 8
PEER_GROUP_KEYS = 2
NEG_INF = float("-inf")


def _cparams(sem):
    return pltpu.CompilerParams(dimension_semantics=sem, vmem_limit_bytes=VMEM_LIMIT)


def _rmsnorm_rows(x, w):
    return x * lax.rsqrt(jnp.mean(x * x, axis=-1, keepdims=True) + EPS) * w


def _ada_kernel(c_ref, w_ref, b_ref, o_ref):
    c = c_ref[...]
    act = c * jax.nn.sigmoid(c)
    o_ref[...] = jnp.dot(act, w_ref[...], preferred_element_type=F32, precision=HIGHEST) + b_ref[...]


def _ada(c, w, b):
    bsz, d = c.shape
    n_out = w.shape[1]
    return pl.pallas_call(
        _ada_kernel,
        grid=(n_out // d,),
        in_specs=[pl.BlockSpec((bsz, d), lambda j: (0, 0)),
                  pl.BlockSpec((d, d), lambda j: (0, j)),
                  pl.BlockSpec((1, d), lambda j: (0, j))],
        out_specs=pl.BlockSpec((bsz, d), lambda j: (0, j)),
        out_shape=jax.ShapeDtypeStruct((bsz, n_out), F32),
        compiler_params=_cparams(("arbitrary",)),
        name="ada",
    )(c, w, b.reshape(1, n_out))


def _inproj_kernel(x_ref, nw_ref, sc_ref, sh_ref, w_ref,
                   att_ref, mqk_ref, mv_ref, mo_ref, g_ref, if_ref):
    y = _rmsnorm_rows(x_ref[...], nw_ref[...])
    u = (y * (1.0 + sc_ref[...]) + sh_ref[...]).astype(BF16)

    def proj(lo, width):
        return jnp.dot(u, w_ref[:, lo:lo + width], preferred_element_type=F32)

    att_ref[...] = proj(OFF_ATT, ATT_COLS).astype(BF16)
    mqk_ref[...] = proj(OFF_MQK, MQK_COLS).astype(BF16)
    mv_ref[...] = proj(OFF_MV, MLSTM_W).astype(BF16)
    mo_ref[...] = proj(OFF_MO, MLSTM_W).astype(BF16)
    g_ref[...] = proj(OFF_G, GATE_COLS).astype(BF16)
    if_ref[...] = proj(OFF_IF, IF_COLS)


def _inproj(x, norm_w, sc, sh, w_perm):
    bsz, s, d = x.shape
    tm = TOK_TILE
    tok = lambda width: pl.BlockSpec((None, tm, width), lambda b, i: (b, i, 0))
    vec = pl.BlockSpec((None, 1, d), lambda b, i: (b, 0, 0))
    widths = (ATT_COLS, MQK_COLS, MLSTM_W, MLSTM_W, GATE_COLS, IF_COLS)
    dtypes = (BF16, BF16, BF16, BF16, BF16, F32)
    return pl.pallas_call(
        _inproj_kernel,
        grid=(bsz, s // tm),
        in_specs=[tok(d),
                  pl.BlockSpec((1, d), lambda b, i: (0, 0)),
                  vec, vec,
                  pl.BlockSpec((d, IN_COLS), lambda b, i: (0, 0))],
        out_specs=[tok(w) for w in widths],
        out_shape=[jax.ShapeDtypeStruct((bsz, s, w), dt) for w, dt in zip(widths, dtypes)],
        compiler_params=_cparams(("parallel", "parallel")),
        name="inproj",
    )(x, norm_w.reshape(1, d), sc, sh, w_perm)


def _attn_kernel(sink_ref, cur_ref, prev_ref, cosc_ref, sinc_ref, cosp_ref, sinp_ref, o_ref):
    blk = ATT_BLOCK
    n = pl.program_id(1)
    lane = lax.broadcasted_iota(jnp.int32, (blk, LANES), 1)
    first_half = (lane & (ATT_HEAD_DIM - 1)) < (ATT_HEAD_DIM // 2)
    low = lane < ATT_HEAD_DIM

    def rope(xf, cos, sin):
        rot = jnp.where(first_half, pltpu.roll(xf, LANES - ATT_HEAD_DIM // 2, 1),
                        pltpu.roll(xf, ATT_HEAD_DIM // 2, 1))
        return xf * cos + rot * sin

    def dup(xf, g):
        sw = pltpu.roll(xf, ATT_HEAD_DIM, 1)
        return jnp.where(low, xf, sw) if g == 0 else jnp.where(low, sw, xf)

    cosc, sinc = cosc_ref[...], sinc_ref[...]
    cosp, sinp = cosp_ref[...], sinp_ref[...]
    k_cur = rope(cur_ref[:, ATT_Q_W:ATT_Q_W + ATT_KV_W].astype(F32), cosc, sinc)
    k_prev = rope(prev_ref[:, ATT_Q_W:ATT_Q_W + ATT_KV_W].astype(F32), cosp, sinp)
    v_cur = cur_ref[:, ATT_Q_W + ATT_KV_W:ATT_COLS].astype(F32)
    v_prev = prev_ref[:, ATT_Q_W + ATT_KV_W:ATT_COLS].astype(F32)

    qi = lax.broadcasted_iota(jnp.int32, (blk, 2 * blk), 0)
    ki = lax.broadcasted_iota(jnp.int32, (blk, 2 * blk), 1)
    q_lim = jnp.where(n > 0, qi, 2 * blk)
    mask = ((ki < blk) & (ki > q_lim)) | ((ki >= blk) & ((ki - blk) <= qi))

    scale = ATT_HEAD_DIM ** -0.5
    group = ATT_HEADS // ATT_KV_HEADS
    for g in range(ATT_KV_HEADS):
        kk = jnp.concatenate([dup(k_prev, g), dup(k_cur, g)], axis=0).astype(BF16)
        vv = jnp.concatenate([dup(v_prev, g), dup(v_cur, g)], axis=0).astype(BF16)
        for pair in range(group // 2):
            j = g * (group // 2) + pair
            qp = rope(cur_ref[:, j * LANES:(j + 1) * LANES].astype(F32), cosc, sinc) * scale
            outs = []
            for half in range(2):
                head = 2 * j + half
                qh = jnp.where(low, qp, 0.0) if half == 0 else jnp.where(low, 0.0, qp)
                s = lax.dot_general(qh.astype(BF16), kk, (((1,), (1,)), ((), ())),
                                    preferred_element_type=F32)
                s = jnp.where(mask, s, NEG_INF)
                sink = sink_ref[head]
                m = jnp.maximum(jnp.max(s, axis=-1, keepdims=True), sink)
                p = jnp.exp(s - m)
                denom = jnp.sum(p, axis=-1, keepdims=True) + jnp.exp(sink - m)
                o = jnp.dot(p.astype(BF16), vv, preferred_element_type=F32)
                outs.append(o / denom)
            o_ref[:, j * LANES:(j + 1) * LANES] = jnp.where(low, outs[0], outs[1]).astype(BF16)


def _attention(att_in, sinks, cos_t, sin_t):
    bsz, s, _ = att_in.shape
    blk = ATT_BLOCK
    nb = s // blk
    prev = lambda n: jnp.maximum(n - 1, 0)
    return pl.pallas_call(
        _attn_kernel,
        grid=(bsz, nb),
        in_specs=[pl.BlockSpec(memory_space=pltpu.SMEM),
                  pl.BlockSpec((None, blk, ATT_COLS), lambda b, n: (b, n, 0)),
                  pl.BlockSpec((None, blk, ATT_COLS), lambda b, n: (b, prev(n), 0)),
                  pl.BlockSpec((blk, LANES), lambda b, n: (n, 0)),
                  pl.BlockSpec((blk, LANES), lambda b, n: (n, 0)),
                  pl.BlockSpec((blk, LANES), lambda b, n: (prev(n), 0)),
                  pl.BlockSpec((blk, LANES), lambda b, n: (prev(n), 0))],
        out_specs=pl.BlockSpec((None, blk, ATT_Q_W), lambda b, n: (b, n, 0)),
        out_shape=jax.ShapeDtypeStruct((bsz, s, ATT_Q_W), BF16),
        compiler_params=_cparams(("parallel", "parallel")),
        name="attn",
    )(sinks, att_in, att_in, cos_t, sin_t, cos_t, sin_t)


def _mlstm_kernel(mqk_ref, mv_ref, mo_ref, if_ref, convw_ref, bias_ref, nw_ref,
                  o_ref, buf_ref, c_ref, n_ref, m_ref):
    L = MLSTM_CHUNK
    d = MLSTM_HEAD_DIM
    tail = SUBLANES

    @pl.when(pl.program_id(1) == 0)
    def _():
        buf_ref[0:tail, :] = jnp.zeros((tail, 2 * MLSTM_W), F32)
        c_ref[...] = jnp.zeros_like(c_ref)
        n_ref[...] = jnp.zeros_like(n_ref)
        m_ref[...] = jnp.zeros_like(m_ref)

    buf_ref[tail:tail + L, :] = mqk_ref[...].astype(F32)
    y = jnp.zeros((L, 2 * MLSTM_W), F32)
    for j in range(CONV_WIDTH):
        off = tail - (CONV_WIDTH - 1) + j
        y = y + convw_ref[j:j + 1, :] * buf_ref[off:off + L, :]
    buf_ref[0:tail, :] = buf_ref[L:L + tail, :]
    y = y * jax.nn.sigmoid(y)
    q_all = y[:, :MLSTM_W]
    k_all = y[:, MLSTM_W:] * (d ** -0.5)

    gates = if_ref[...] + bias_ref[...]
    lf = jnp.minimum(gates, 0.0) - jnp.log1p(jnp.exp(-jnp.abs(gates)))
    row = lax.broadcasted_iota(jnp.int32, (L, L), 0)
    col = lax.broadcasted_iota(jnp.int32, (L, L), 1)
    causal = col <= row
    tril = causal.astype(F32)
    triu = (row <= col).astype(F32)
    bcum_c = jnp.dot(tril, lf, preferred_element_type=F32, precision=HIGHEST)
    gates_t = gates.T
    bcum_r = jnp.dot(lf.T, triu, preferred_element_type=F32, precision=HIGHEST)

    for h in range(MLSTM_HEADS):
        hs = slice(h * d, (h + 1) * d)
        fl = MLSTM_HEADS + h
        q = q_all[:, hs]
        k = k_all[:, hs]
        v = mv_ref[:, hs]
        ic_c = gates[:, h:h + 1]
        ic_r = gates_t[h:h + 1, :]
        b_c = bcum_c[:, fl:fl + 1]
        b_r = bcum_r[fl:fl + 1, :]
        g_tot = b_c[L - 1:L, :]
        c_prev = c_ref[h]
        n_prev = n_ref[h:h + 1, :]
        m_prev = m_ref[h:h + 1, 0:1]

        dmat = jnp.where(causal, b_c - b_r + ic_r, NEG_INF)
        m_inter = b_c + m_prev
        m_out = jnp.maximum(m_inter, jnp.max(dmat, axis=-1, keepdims=True))
        qb = q.astype(BF16)
        kb = k.astype(BF16)
        qk = lax.dot_general(qb, kb, (((1,), (1,)), ((), ())), preferred_element_type=F32)
        w_intra = jnp.exp(dmat - m_out) * qk
        w_inter = jnp.exp(m_inter - m_out)
        num = (jnp.dot(w_intra.astype(BF16), v, preferred_element_type=F32)
               + w_inter * jnp.dot(qb, c_prev.astype(BF16), preferred_element_type=F32))
        den = (jnp.sum(w_intra, axis=-1, keepdims=True)
               + w_inter * jnp.sum(q * n_prev, axis=-1, keepdims=True))
        hid = num / jnp.maximum(jnp.abs(den), jnp.exp(-m_out))
        hid = jax.nn.sigmoid(mo_ref[:, hs].astype(F32)) * hid
        hid = hid * lax.rsqrt(jnp.mean(hid * hid, axis=-1, keepdims=True) + EPS)
        o_ref[:, hs] = (hid * nw_ref[:, hs]).astype(BF16)

        a_c = g_tot - b_c + ic_c
        m_new = jnp.maximum(g_tot + m_prev, jnp.max(a_c, axis=0, keepdims=True))
        decay = jnp.exp(g_tot + m_prev - m_new)
        wk = jnp.exp(a_c - m_new) * k
        c_ref[h] = decay * c_prev + lax.dot_general(
            wk.astype(BF16), v, (((0,), (0,)), ((), ())), preferred_element_type=F32)
        n_ref[h:h + 1, :] = decay * n_prev + jnp.sum(wk, axis=0, keepdims=True)
        m_ref[h:h + 1, :] = jnp.broadcast_to(m_new, (1, LANES))


def _mlstm(mqk, mv, mo, gif, conv_w, bias_row, norm_w):
    bsz, s, _ = mqk.shape
    L = MLSTM_CHUNK
    tok = lambda width: pl.BlockSpec((None, L, width), lambda b, c: (b, c, 0))
    full = lambda shape: pl.BlockSpec(shape, lambda b, c: (0, 0))
    return pl.pallas_call(
        _mlstm_kernel,
        grid=(bsz, s // L),
        in_specs=[tok(MQK_COLS), tok(MLSTM_W), tok(MLSTM_W), tok(IF_COLS),
                  full((CONV_WIDTH, MQK_COLS)), full((1, IF_COLS)), full((1, MLSTM_W))],
        out_specs=tok(MLSTM_W),
        out_shape=jax.ShapeDtypeStruct((bsz, s, MLSTM_W), BF16),
        scratch_shapes=[pltpu.VMEM((L + SUBLANES, MQK_COLS), F32),
                        pltpu.VMEM((MLSTM_HEADS, MLSTM_HEAD_DIM, MLSTM_HEAD_DIM), F32),
                        pltpu.VMEM((SUBLANES, MLSTM_HEAD_DIM), F32),
                        pltpu.VMEM((SUBLANES, LANES), F32)],
        compiler_params=_cparams(("arbitrary", "arbitrary")),
        name="mlstm",
    )(mqk, mv, mo, gif, conv_w, bias_row, norm_w.reshape(1, MLSTM_W))


def _merge_kernel(x_ref, att_ref, mls_ref, g_ref, gate_ref, wa_ref, wm_ref, wo_ref, h_ref):
    a = jnp.dot(att_ref[...], wa_ref[...], preferred_element_type=F32)
    m = jnp.dot(mls_ref[...], wm_ref[...], preferred_element_type=F32)
    ga = g_ref[:, :D_MODEL].astype(F32)
    gm = g_ref[:, D_MODEL:].astype(F32)
    merged = jax.nn.sigmoid(ga) * a + jax.nn.sigmoid(gm) * m
    y = jnp.dot(merged.astype(BF16), wo_ref[...], preferred_element_type=F32)
    h_ref[...] = x_ref[...] + gate_ref[...] * y


def _merge(x, att, mls, g, gate1, w_att, w_mls, w_out):
    bsz, s, d = x.shape
    tm = TOK_TILE
    tok = lambda width: pl.BlockSpec((None, tm, width), lambda b, i: (b, i, 0))
    full = lambda shape: pl.BlockSpec(shape, lambda b, i: (0, 0))
    return pl.pallas_call(
        _merge_kernel,
        grid=(bsz, s // tm),
        in_specs=[tok(d), tok(ATT_Q_W), tok(MLSTM_W), tok(GATE_COLS),
                  pl.BlockSpec((None, 1, d), lambda b, i: (b, 0, 0)),
                  full((ATT_Q_W, d)), full((MLSTM_W, d)), full((d, d))],
        out_specs=tok(d),
        out_shape=jax.ShapeDtypeStruct((bsz, s, d), F32),
        compiler_params=_cparams(("parallel", "parallel")),
        name="merge",
    )(x, att, mls, g, gate1, w_att, w_mls, w_out)


def _top16(s, iota_f):
    n_keys = s.shape[0]
    work = s
    rank = jnp.full(s.shape, float(PEER_TOPK), F32)
    vals = []
    for a in range(PEER_TOPK):
        m = jnp.max(work, axis=0, keepdims=True)
        idx = jnp.min(jnp.where(work == m, iota_f, float(n_keys)), axis=0, keepdims=True)
        sel = iota_f == idx
        rank = jnp.where(sel, float(a), rank)
        work = jnp.where(sel, NEG_INF, work)
        vals.append(m)
    return vals, rank


def _route_exact(s1, s2):
    k = PEER_TOPK
    nk = PEER_N_KEYS
    iota_keys = lax.broadcasted_iota(jnp.int32, (nk, LANES), 0).astype(F32)
    iota_k = lax.broadcasted_iota(jnp.int32, (k, LANES), 0).astype(F32)
    vals0, rank0 = _top16(s1, iota_keys)
    vals1, rank1 = _top16(s2, iota_keys)
    v0 = jnp.concatenate(vals0, axis=0)

    cnt = jnp.zeros((k, LANES), F32)
    front = v0 + vals1[0]
    c_max = vals0[0] + vals1[0]
    z = jnp.zeros((1, LANES), F32)
    for _ in range(k):
        m = jnp.max(front, axis=0, keepdims=True)
        idx = jnp.min(jnp.where(front == m, iota_k, float(k)), axis=0, keepdims=True)
        sel = iota_k == idx
        cnt = cnt + jnp.where(sel, 1.0, 0.0)
        z = z + jnp.exp(m - c_max)
        taken = jnp.sum(jnp.where(sel, cnt, 0.0), axis=0, keepdims=True)
        nxt = jnp.full((1, LANES), NEG_INF, F32)
        for b in range(1, k):
            nxt = jnp.where(taken == float(b), vals1[b], nxt)
        front = jnp.where(sel, v0 + nxt, front)

    cnt_keys = jnp.zeros((nk, LANES), F32)
    for a in range(k):
        cnt_keys = jnp.where(rank0 == float(a), cnt[a:a + 1, :], cnt_keys)
    return rank1, jnp.exp(s2 - vals1[0]), cnt_keys, jnp.exp(s1 - vals0[0]) / z


def _sort_network(n):
    def merge(lo, hi, r):
        step = r * 2
        if step < hi - lo:
            yield from merge(lo, hi, step)
            yield from merge(lo + r, hi, step)
            yield from [(i, i + r) for i in range(lo + r, hi - r, step)]
        else:
            yield (lo, lo + r)

    def sort(lo, hi):
        if hi - lo >= 1:
            mid = lo + (hi - lo) // 2
            yield from sort(lo, mid)
            yield from sort(mid + 1, hi)
            yield from merge(lo, hi, 1)

    return list(sort(0, n - 1))


def _top16_values(x):
    n = PEER_TOPK
    x = list(x) + [None] * (n - len(x))

    def top(a, b):
        return b if a is None else a if b is None else jnp.maximum(a, b)

    def cmpx(i, j):
        a, b = x[i], x[j]
        if b is None:
            return
        if a is None:
            x[i], x[j] = b, None
        else:
            x[i], x[j] = jnp.maximum(a, b), jnp.minimum(a, b)

    for i, j in _sort_network(n):
        cmpx(i, j)
    shift = SUBLANES // 2
    while shift >= 1:
        y = [None if t is None else pltpu.roll(t, shift, 0) for t in x]
        x = [top(x[i], y[n - 1 - i]) for i in range(n)]
        gap = n // 2
        while gap >= 1:
            for i in range(n):
                if i & gap == 0:
                    cmpx(i, i + gap)
            gap //= 2
        shift //= 2
    return x


def _route_fast(s1, s2):
    k = PEER_TOPK
    nk = PEER_N_KEYS
    tiles = nk // SUBLANES
    s1r = s1.reshape(tiles, SUBLANES, LANES)
    s2r = s2.reshape(tiles, SUBLANES, LANES)
    v0 = _top16_values([s1r[t] for t in range(tiles)])
    v1 = _top16_values([s2r[t] for t in range(tiles)])
    sub = lax.broadcasted_iota(jnp.int32, (SUBLANES, LANES), 0)
    one = jnp.ones((SUBLANES, LANES), F32)
    zero = jnp.zeros((SUBLANES, LANES), F32)

    tie = zero
    for v, sr in ((v0, s1r), (v1, s2r)):
        for a in range(k - 1):
            tie = jnp.where(v[a] == v[a + 1], one, tie)
        above = jnp.sum(jnp.where(sr >= v[k - 1], 1.0, 0.0), axis=0)
        tie = jnp.where(jnp.sum(above, axis=0, keepdims=True) != float(k), one, tie)

    def by_sublane(vals):
        out = vals[SUBLANES - 1]
        for a in range(SUBLANES - 2, -1, -1):
            out = jnp.where(sub == a, vals[a], out)
        return out

    v0_lo, v0_hi, v1_hi = by_sublane(v0[:SUBLANES]), by_sublane(v0[SUBLANES:]), by_sublane(v1[SUBLANES:])
    cand = [v0_lo + v1[0], v0_hi + v1[0]]
    for b in range(1, SUBLANES):
        cand.append(jnp.where(sub < (k // (b + 1)), v0_lo + v1[b], NEG_INF))
    cand.append(v0[0] + v1_hi)
    best = _top16_values(cand)
    z = one
    for a in range(1, k):
        z = z + jnp.exp(best[a] - best[0])
    picked = [jnp.where(t >= best[k - 1], one, zero) for t in cand]
    cnt_lo = picked[0]
    for t in picked[2:-1]:
        cnt_lo = cnt_lo + t
    cnt_lo = cnt_lo + jnp.where(sub == 0, jnp.sum(picked[-1], axis=0, keepdims=True), zero)
    cnt_hi = picked[1]
    total = jnp.sum(cnt_lo + cnt_hi, axis=0, keepdims=True)
    tie = jnp.where(total != float(k), one, tie)

    cnt_keys = jnp.zeros((tiles, SUBLANES, LANES), F32)
    for a in range(k):
        src = cnt_lo if a < SUBLANES else cnt_hi
        row = a % SUBLANES
        cnt_a = jnp.broadcast_to(src[row:row + 1, :], (SUBLANES, LANES))
        cnt_keys = jnp.where(s1r == v0[a], cnt_a, cnt_keys)
    rank1 = jnp.full((tiles, SUBLANES, LANES), float(k), F32)
    for a in range(k - 1, -1, -1):
        rank1 = jnp.where(v1[a] <= s2r, float(a), rank1)
    e2 = jnp.exp(s2r - v1[0])
    c1 = jnp.exp(s1r - v0[0]) / z
    flat = lambda t: t.reshape(nk, LANES)
    return flat(rank1), flat(e2), flat(cnt_keys), flat(c1), tie


def _fold_keys_kernel(keys_ref, wq_ref, o_ref):
    o_ref[...] = lax.dot_general(keys_ref[...], wq_ref[...], (((1,), (1,)), ((), ())),
                                 preferred_element_type=F32, precision=HIGHEST).astype(BF16)


def _fold_keys(keys, wq):
    d, qdim = wq.shape
    nk, half = keys.shape[1:]
    return pl.pallas_call(
        _fold_keys_kernel,
        grid=(qdim // half,),
        in_specs=[pl.BlockSpec((None, nk, half), lambda j: (j % 2, 0, 0)),
                  pl.BlockSpec((d, half), lambda j: (0, j))],
        out_specs=pl.BlockSpec((nk, d), lambda j: (j, 0)),
        out_shape=jax.ShapeDtypeStruct((qdim // half * nk, d), BF16),
        compiler_params=_cparams(("arbitrary",)),
        name="fold_keys",
    )(keys, wq)


def _route_kernel(h_ref, nw_ref, sc_ref, sh_ref, ws_ref,
                  r2_ref, e2_ref, cnt_ref, c1_ref, score_ref):
    u2 = (_rmsnorm_rows(h_ref[...], nw_ref[...]) * (1.0 + sc_ref[...]) + sh_ref[...]).astype(BF16)
    score_ref[...] = lax.dot_general(ws_ref[...], u2, (((1,), (1,)), ((), ())),
                                     preferred_element_type=F32)
    n_col = h_ref.shape[0] // LANES
    nk = PEER_N_KEYS

    def body(it, carry):
        head = it // n_col
        tcol = it % n_col
        lanes = pl.ds(pl.multiple_of(tcol * LANES, LANES), LANES)
        rows0 = pl.ds(pl.multiple_of(head * (2 * nk), nk), nk)
        rows1 = pl.ds(pl.multiple_of(head * (2 * nk) + nk, nk), nk)
        s1 = score_ref[rows0, lanes]
        s2 = score_ref[rows1, lanes]

        def store(rank1, e2, cnt_keys, c1):
            r2_ref[head, tcol] = pltpu.bitcast(rank1.astype(BF16), jnp.int32)
            e2_ref[head, tcol] = pltpu.bitcast(e2.astype(BF16), jnp.int32)
            cnt_ref[head, tcol] = cnt_keys
            c1_ref[head, tcol] = c1

        rank1, e2, cnt_keys, c1, tie = _route_fast(s1, s2)
        store(rank1, e2, cnt_keys, c1)

        @pl.when(jnp.max(tie) > 0.0)
        def _():
            store(*_route_exact(score_ref[rows0, lanes], score_ref[rows1, lanes]))

        return carry

    lax.fori_loop(0, PEER_HEADS * n_col, body, 0)


def _route(h, norm_w, sc, sh, w_score):
    bsz, s, d = h.shape
    tt = PEER_TOK
    sel = lambda rows: pl.BlockSpec((None, PEER_HEADS, tt // LANES, rows, LANES), lambda b, t: (b, 0, t, 0, 0))
    out = lambda rows, dt: jax.ShapeDtypeStruct((bsz, PEER_HEADS, s // LANES, rows, LANES), dt)
    packed = PEER_N_KEYS * 2 // 4
    n_scores = w_score.shape[0]
    return pl.pallas_call(
        _route_kernel,
        grid=(bsz, s // tt),
        in_specs=[pl.BlockSpec((None, tt, d), lambda b, t: (b, t, 0)),
                  pl.BlockSpec((1, d), lambda b, t: (0, 0)),
                  pl.BlockSpec((None, 1, d), lambda b, t: (b, 0, 0)),
                  pl.BlockSpec((None, 1, d), lambda b, t: (b, 0, 0)),
                  pl.BlockSpec((n_scores, d), lambda b, t: (0, 0))],
        out_specs=[sel(packed), sel(packed), sel(PEER_N_KEYS), sel(PEER_N_KEYS)],
        out_shape=[out(packed, jnp.int32), out(packed, jnp.int32),
                   out(PEER_N_KEYS, F32), out(PEER_N_KEYS, F32)],
        scratch_shapes=[pltpu.VMEM((n_scores, tt), F32)],
        compiler_params=_cparams(("parallel", "parallel")),
        name="route",
    )(h, norm_w.reshape(1, d), sc, sh, w_score)


def _peer_kernel(h_ref, nw_ref, sc_ref, sh_ref, gate_ref, nf_ref, u_ref, vt_ref,
                 r2_ref, e2_ref, cnt_ref, c1_ref, o_ref,
                 x_ref, at_ref, wg_ref, acc_ref, *, final_norm):
    c = pl.program_id(2)
    tt = h_ref.shape[0]
    nk = PEER_N_KEYS
    ncol = tt // LANES
    gk = PEER_GROUP_KEYS

    @pl.when(c == 0)
    def _():
        u2 = _rmsnorm_rows(h_ref[...], nw_ref[...]) * (1.0 + sc_ref[...]) + sh_ref[...]
        x_ref[...] = u2.astype(BF16)
        acc_ref[...] = jnp.zeros_like(acc_ref)

    pre = lax.dot_general(u_ref[...], x_ref[...], (((1,), (1,)), ((), ())),
                          preferred_element_type=F32)
    for tcol in range(ncol):
        at_ref[tcol] = pre[:, tcol * LANES:(tcol + 1) * LANES]

    def row_tile(ref, hd, tcol, key):
        return jnp.broadcast_to(ref[hd, tcol, key:key + 1, :], (nk, LANES)).astype(BF16)

    def gate_column(tcol, carry):
        for k0 in range(0, PEER_IBLK, gk):
            w = [jnp.zeros((nk, LANES), BF16) for _ in range(gk)]
            for hd in range(PEER_HEADS):
                r2 = pltpu.bitcast(r2_ref[hd, tcol], BF16)
                e2 = pltpu.bitcast(e2_ref[hd, tcol], BF16)
                for k in range(gk):
                    sel = r2 < row_tile(cnt_ref, hd, tcol, k0 + k)
                    w[k] = w[k] + (jnp.where(sel, e2, jnp.zeros_like(e2))
                                   * row_tile(c1_ref, hd, tcol, k0 + k))
            for k in range(gk):
                rows = slice((k0 + k) * nk, (k0 + k + 1) * nk)
                a = at_ref[tcol, rows, :]
                act = 0.5 * a * (1.0 + lax.erf(a * np.float32(math.sqrt(0.5))))
                wg_ref[tcol, rows, :] = w[k] * act.astype(BF16)
        return carry

    lax.fori_loop(0, ncol, gate_column, 0)
    wg = jnp.concatenate([wg_ref[tcol] for tcol in range(ncol)], axis=1)
    acc_ref[...] += jnp.dot(vt_ref[...], wg, preferred_element_type=F32)

    @pl.when(c == pl.num_programs(2) - 1)
    def _():
        hh = h_ref[...] + gate_ref[...] * acc_ref[...].T
        o_ref[...] = _rmsnorm_rows(hh, nf_ref[...]) if final_norm else hh


def _peer(h, norm_w, sc, sh, gate2, normf_w, u_bf, vt_bf, r2, e2, cnt, c1, final_norm):
    bsz, s, d = h.shape
    tt = PEER_TOK
    ec = PEER_IBLK * PEER_N_KEYS
    n_exp = u_bf.shape[0]
    vec = pl.BlockSpec((1, d), lambda b, t, c: (0, 0))
    bvec = pl.BlockSpec((None, 1, d), lambda b, t, c: (b, 0, 0))
    ncol = tt // LANES
    selj = pl.BlockSpec((None, PEER_HEADS, ncol, PEER_N_KEYS // 2, LANES), lambda b, t, c: (b, 0, t, 0, 0))
    seli = pl.BlockSpec((None, PEER_HEADS, ncol, PEER_IBLK, LANES), lambda b, t, c: (b, 0, t, c, 0))
    return pl.pallas_call(
        functools.partial(_peer_kernel, final_norm=final_norm),
        grid=(bsz, s // tt, n_exp // ec),
        in_specs=[pl.BlockSpec((None, tt, d), lambda b, t, c: (b, t, 0)),
                  vec, bvec, bvec, bvec, vec,
                  pl.BlockSpec((ec, d), lambda b, t, c: (c, 0)),
                  pl.BlockSpec((d, ec), lambda b, t, c: (0, c)),
                  selj, selj, seli, seli],
        out_specs=pl.BlockSpec((None, tt, d), lambda b, t, c: (b, t, 0)),
        out_shape=jax.ShapeDtypeStruct((bsz, s, d), F32),
        scratch_shapes=[pltpu.VMEM((tt, d), BF16),
                        pltpu.VMEM((ncol, ec, LANES), F32),
                        pltpu.VMEM((ncol, ec, LANES), BF16),
                        pltpu.VMEM((d, tt), F32)],
        compiler_params=_cparams(("parallel", "parallel", "arbitrary")),
        name="peer",
    )(h, norm_w.reshape(1, d), sc, sh, gate2, normf_w.reshape(1, d), u_bf, vt_bf, r2, e2, cnt, c1)


def _rope_tables(s):
    half = ATT_HEAD_DIM // 2
    inv_freq = ROPE_THETA ** (-jnp.arange(half, dtype=F32) * 2.0 / ATT_HEAD_DIM)
    ang = jnp.arange(s, dtype=F32)[:, None] * inv_freq[None, :]
    cos, sin = jnp.cos(ang), jnp.sin(ang)
    reps = LANES // ATT_HEAD_DIM
    cos_t = jnp.tile(jnp.concatenate([cos, cos], axis=-1), (1, reps))
    sin_t = jnp.tile(jnp.concatenate([-sin, sin], axis=-1), (1, reps))
    return cos_t, sin_t


def _permute_in_proj(w_in):
    n_if = 2 * MLSTM_HEADS
    lo = ATT_COLS + MQK_COLS + 2 * MLSTM_W
    pad = jnp.zeros((w_in.shape[0], IF_COLS - n_if), w_in.dtype)
    return jnp.concatenate([w_in[:, :lo], w_in[:, lo + n_if:], w_in[:, lo:lo + n_if], pad], axis=1)


def kernel(x, c, w_ada, b_ada, norm1_w, w_in, conv_w, att_sinks, i_bias, f_bias, mlstm_norm_w,
           w_att_branch, w_mlstm_branch, w_out, norm2_w, peer_w_query, peer_sub_keys, peer_u,
           peer_v, norm_f_w):
    depth = w_ada.shape[0]
    bsz, s, d = x.shape
    cos_t, sin_t = _rope_tables(s)
    h = x
    for l in range(depth):
        mod = _ada(c, w_ada[l], b_ada[l]).reshape(bsz, 6, 1, d)
        sh1, sc1, g1, sh2, sc2, g2 = (mod[:, i] for i in range(6))

        w_perm = _permute_in_proj(w_in[l]).astype(BF16)
        att_in, mqk, mv, mo, gates, gif = _inproj(h, norm1_w[l], sc1, sh1, w_perm)
        att = _attention(att_in, att_sinks[l], cos_t, sin_t)
        bias_row = jnp.concatenate(
            [i_bias[l], f_bias[l], jnp.zeros((IF_COLS - 2 * MLSTM_HEADS,), F32)]).reshape(1, IF_COLS)
        mls = _mlstm(mqk, mv, mo, gif, conv_w[l], bias_row, mlstm_norm_w[l])
        h = _merge(h, att, mls, gates, g1, w_att_branch[l].astype(BF16),
                   w_mlstm_branch[l].astype(BF16), w_out[l].astype(BF16))

        w_score = _fold_keys(peer_sub_keys[l], peer_w_query[l])
        r2, e2, cnt, c1 = _route(h, norm2_w[l], sc2, sh2, w_score)
        h = _peer(h, norm2_w[l], sc2, sh2, g2, norm_f_w, peer_u[l].astype(BF16),
                  peer_v[l].T.astype(BF16), r2, e2, cnt, c1, final_norm=(l == depth - 1))
    return h
```

```python
import functools
import math

import numpy as np
import jax
import jax.numpy as jnp
from jax import lax
from jax.experimental import pallas as pl
from jax.experimental.pallas import tpu as pltpu

F32 = jnp.float32
BF16 = jnp.bfloat16
HIGHEST = lax.Precision.HIGHEST

D_MODEL = 1024
ATT_HEADS = 8
ATT_KV_HEADS = 2
ATT_HEAD_DIM = 64
ATT_BLOCK = 128
ROPE_THETA = 10000.0
MLSTM_HEADS = 4
MLSTM_HEAD_DIM = 128
MLSTM_CHUNK = 128
CONV_WIDTH = 4
ATT_Q_W = ATT_HEADS * ATT_HEAD_DIM
ATT_KV_W = ATT_KV_HEADS * ATT_HEAD_DIM
MLSTM_W = MLSTM_HEADS * MLSTM_HEAD_DIM
PEER_HEADS = 8
PEER_N_KEYS = 128
PEER_HALF = 128
PEER_TOPK = 16
EPS = 1e-6

LANES = 128
SUBLANES = 8
BF16_ROWS = 16
VMEM_LIMIT = 48 * 1024 * 1024

ATT_COLS = ATT_Q_W + 2 * ATT_KV_W
MQK_COLS = 2 * MLSTM_W
GATE_COLS = 2 * D_MODEL
IF_COLS = LANES
OFF_ATT = 0
OFF_MQK = OFF_ATT + ATT_COLS
OFF_MV = OFF_MQK + MQK_COLS
OFF_MO = OFF_MV + MLSTM_W
OFF_G = OFF_MO + MLSTM_W
OFF_IF = OFF_G + GATE_COLS
IN_COLS = OFF_IF + IF_COLS

TOK_TILE = 512
PEER_TOK = 512
PEER_IBLK = 8
PEER_TRIPS = 2
PEER_GROUP_KEYS = 1
NEG_INF = float("-inf")


def _cparams(sem):
    return pltpu.CompilerParams(dimension_semantics=sem, vmem_limit_bytes=VMEM_LIMIT)


def _rmsnorm_rows(x, w):
    return x * lax.rsqrt(jnp.mean(x * x, axis=-1, keepdims=True) + EPS) * w


def _ada_kernel(c_ref, w_ref, b_ref, o_ref):
    c = c_ref[...]
    act = c * jax.nn.sigmoid(c)
    o_ref[...] = jnp.dot(act, w_ref[...], preferred_element_type=F32, precision=HIGHEST) + b_ref[...]


def _ada(c, w, b):
    bsz, d = c.shape
    n_out = w.shape[1]
    return pl.pallas_call(
        _ada_kernel,
        grid=(n_out // d,),
        in_specs=[pl.BlockSpec((bsz, d), lambda j: (0, 0)),
                  pl.BlockSpec((d, d), lambda j: (0, j)),
                  pl.BlockSpec((1, d), lambda j: (0, j))],
        out_specs=pl.BlockSpec((bsz, d), lambda j: (0, j)),
        out_shape=jax.ShapeDtypeStruct((bsz, n_out), F32),
        compiler_params=_cparams(("arbitrary",)),
        name="ada",
    )(c, w, b.reshape(1, n_out))


def _inproj_kernel(x_ref, nw_ref, sc_ref, sh_ref, w_ref,
                   att_ref, mqk_ref, mv_ref, mo_ref, g_ref, if_ref):
    y = _rmsnorm_rows(x_ref[...], nw_ref[...])
    u = (y * (1.0 + sc_ref[...]) + sh_ref[...]).astype(BF16)

    def proj(lo, width):
        return jnp.dot(u, w_ref[:, lo:lo + width], preferred_element_type=F32)

    att_ref[...] = proj(OFF_ATT, ATT_COLS).astype(BF16)
    mqk_ref[...] = proj(OFF_MQK, MQK_COLS).astype(BF16)
    mv_ref[...] = proj(OFF_MV, MLSTM_W).astype(BF16)
    mo_ref[...] = proj(OFF_MO, MLSTM_W).astype(BF16)
    g_ref[...] = proj(OFF_G, GATE_COLS).astype(BF16)
    if_ref[...] = proj(OFF_IF, IF_COLS)


def _inproj(x, norm_w, sc, sh, w_perm):
    bsz, s, d = x.shape
    tm = TOK_TILE
    tok = lambda width: pl.BlockSpec((None, tm, width), lambda b, i: (b, i, 0))
    vec = pl.BlockSpec((None, 1, d), lambda b, i: (b, 0, 0))
    widths = (ATT_COLS, MQK_COLS, MLSTM_W, MLSTM_W, GATE_COLS, IF_COLS)
    dtypes = (BF16, BF16, BF16, BF16, BF16, F32)
    return pl.pallas_call(
        _inproj_kernel,
        grid=(bsz, s // tm),
        in_specs=[tok(d),
                  pl.BlockSpec((1, d), lambda b, i: (0, 0)),
                  vec, vec,
                  pl.BlockSpec((d, IN_COLS), lambda b, i: (0, 0))],
        out_specs=[tok(w) for w in widths],
        out_shape=[jax.ShapeDtypeStruct((bsz, s, w), dt) for w, dt in zip(widths, dtypes)],
        compiler_params=_cparams(("parallel", "parallel")),
        name="inproj",
    )(x, norm_w.reshape(1, d), sc, sh, w_perm)


def _attn_kernel(sink_ref, cur_ref, prev_ref, cosc_ref, sinc_ref, cosp_ref, sinp_ref, o_ref):
    blk = ATT_BLOCK
    n = pl.program_id(1)
    lane = lax.broadcasted_iota(jnp.int32, (blk, LANES), 1)
    first_half = (lane & (ATT_HEAD_DIM - 1)) < (ATT_HEAD_DIM // 2)
    low = lane < ATT_HEAD_DIM

    def rope(xf, cos, sin):
        rot = jnp.where(first_half, pltpu.roll(xf, LANES - ATT_HEAD_DIM // 2, 1),
                        pltpu.roll(xf, ATT_HEAD_DIM // 2, 1))
        return xf * cos + rot * sin

    def dup(xf, g):
        sw = pltpu.roll(xf, ATT_HEAD_DIM, 1)
        return jnp.where(low, xf, sw) if g == 0 else jnp.where(low, sw, xf)

    cosc, sinc = cosc_ref[...], sinc_ref[...]
    cosp, sinp = cosp_ref[...], sinp_ref[...]
    k_cur = rope(cur_ref[:, ATT_Q_W:ATT_Q_W + ATT_KV_W].astype(F32), cosc, sinc)
    k_prev = rope(prev_ref[:, ATT_Q_W:ATT_Q_W + ATT_KV_W].astype(F32), cosp, sinp)
    v_cur = cur_ref[:, ATT_Q_W + ATT_KV_W:ATT_COLS].astype(F32)
    v_prev = prev_ref[:, ATT_Q_W + ATT_KV_W:ATT_COLS].astype(F32)

    qi = lax.broadcasted_iota(jnp.int32, (blk, 2 * blk), 0)
    ki = lax.broadcasted_iota(jnp.int32, (blk, 2 * blk), 1)
    q_lim = jnp.where(n > 0, qi, 2 * blk)
    mask = ((ki < blk) & (ki > q_lim)) | ((ki >= blk) & ((ki - blk) <= qi))

    scale = ATT_HEAD_DIM ** -0.5
    group = ATT_HEADS // ATT_KV_HEADS
    for g in range(ATT_KV_HEADS):
        kk = jnp.concatenate([dup(k_prev, g), dup(k_cur, g)], axis=0).astype(BF16)
        vv = jnp.concatenate([dup(v_prev, g), dup(v_cur, g)], axis=0).astype(BF16)
        for pair in range(group // 2):
            j = g * (group // 2) + pair
            qp = rope(cur_ref[:, j * LANES:(j + 1) * LANES].astype(F32), cosc, sinc) * scale
            outs = []
            for half in range(2):
                head = 2 * j + half
                qh = jnp.where(low, qp, 0.0) if half == 0 else jnp.where(low, 0.0, qp)
                s = lax.dot_general(qh.astype(BF16), kk, (((1,), (1,)), ((), ())),
                                    preferred_element_type=F32)
                s = jnp.where(mask, s, NEG_INF)
                sink = sink_ref[head]
                m = jnp.maximum(jnp.max(s, axis=-1, keepdims=True), sink)
                p = jnp.exp(s - m)
                denom = jnp.sum(p, axis=-1, keepdims=True) + jnp.exp(sink - m)
                o = jnp.dot(p.astype(BF16), vv, preferred_element_type=F32)
                outs.append(o / denom)
            o_ref[:, j * LANES:(j + 1) * LANES] = jnp.where(low, outs[0], outs[1]).astype(BF16)


def _attention(att_in, sinks, cos_t, sin_t):
    bsz, s, _ = att_in.shape
    blk = ATT_BLOCK
    nb = s // blk
    prev = lambda n: jnp.maximum(n - 1, 0)
    return pl.pallas_call(
        _attn_kernel,
        grid=(bsz, nb),
        in_specs=[pl.BlockSpec(memory_space=pltpu.SMEM),
                  pl.BlockSpec((None, blk, ATT_COLS), lambda b, n: (b, n, 0)),
                  pl.BlockSpec((None, blk, ATT_COLS), lambda b, n: (b, prev(n), 0)),
                  pl.BlockSpec((blk, LANES), lambda b, n: (n, 0)),
                  pl.BlockSpec((blk, LANES), lambda b, n: (n, 0)),
                  pl.BlockSpec((blk, LANES), lambda b, n: (prev(n), 0)),
                  pl.BlockSpec((blk, LANES), lambda b, n: (prev(n), 0))],
        out_specs=pl.BlockSpec((None, blk, ATT_Q_W), lambda b, n: (b, n, 0)),
        out_shape=jax.ShapeDtypeStruct((bsz, s, ATT_Q_W), BF16),
        compiler_params=_cparams(("parallel", "parallel")),
        name="attn",
    )(sinks, att_in, att_in, cos_t, sin_t, cos_t, sin_t)


def _mlstm_kernel(mqk_ref, mv_ref, mo_ref, if_ref, convw_ref, bias_ref, nw_ref,
                  o_ref, buf_ref, c_ref, n_ref, m_ref):
    L = MLSTM_CHUNK
    d = MLSTM_HEAD_DIM
    tail = SUBLANES

    @pl.when(pl.program_id(1) == 0)
    def _():
        buf_ref[0:tail, :] = jnp.zeros((tail, 2 * MLSTM_W), F32)
        c_ref[...] = jnp.zeros_like(c_ref)
        n_ref[...] = jnp.zeros_like(n_ref)
        m_ref[...] = jnp.zeros_like(m_ref)

    buf_ref[tail:tail + L, :] = mqk_ref[...].astype(F32)
    y = jnp.zeros((L, 2 * MLSTM_W), F32)
    for j in range(CONV_WIDTH):
        off = tail - (CONV_WIDTH - 1) + j
        y = y + convw_ref[j:j + 1, :] * buf_ref[off:off + L, :]
    buf_ref[0:tail, :] = buf_ref[L:L + tail, :]
    y = y * jax.nn.sigmoid(y)
    q_all = y[:, :MLSTM_W]
    k_all = y[:, MLSTM_W:] * (d ** -0.5)

    gates = if_ref[...] + bias_ref[...]
    lf = jnp.minimum(gates, 0.0) - jnp.log1p(jnp.exp(-jnp.abs(gates)))
    row = lax.broadcasted_iota(jnp.int32, (L, L), 0)
    col = lax.broadcasted_iota(jnp.int32, (L, L), 1)
    causal = col <= row
    tril = causal.astype(F32)
    triu = (row <= col).astype(F32)
    bcum_c = jnp.dot(tril, lf, preferred_element_type=F32, precision=HIGHEST)
    gates_t = gates.T
    bcum_r = jnp.dot(lf.T, triu, preferred_element_type=F32, precision=HIGHEST)

    for h in range(MLSTM_HEADS):
        hs = slice(h * d, (h + 1) * d)
        fl = MLSTM_HEADS + h
        q = q_all[:, hs]
        k = k_all[:, hs]
        v = mv_ref[:, hs]
        ic_c = gates[:, h:h + 1]
        ic_r = gates_t[h:h + 1, :]
        b_c = bcum_c[:, fl:fl + 1]
        b_r = bcum_r[fl:fl + 1, :]
        g_tot = b_c[L - 1:L, :]
        c_prev = c_ref[h]
        n_prev = n_ref[h:h + 1, :]
        m_prev = m_ref[h:h + 1, 0:1]

        dmat = jnp.where(causal, b_c - b_r + ic_r, NEG_INF)
        m_inter = b_c + m_prev
        m_out = jnp.maximum(m_inter, jnp.max(dmat, axis=-1, keepdims=True))
        qb = q.astype(BF16)
        kb = k.astype(BF16)
        qk = lax.dot_general(qb, kb, (((1,), (1,)), ((), ())), preferred_element_type=F32)
        w_intra = jnp.exp(dmat - m_out) * qk
        w_inter = jnp.exp(m_inter - m_out)
        num = (jnp.dot(w_intra.astype(BF16), v, preferred_element_type=F32)
               + w_inter * jnp.dot(qb, c_prev.astype(BF16), preferred_element_type=F32))
        den = (jnp.sum(w_intra, axis=-1, keepdims=True)
               + w_inter * jnp.sum(q * n_prev, axis=-1, keepdims=True))
        hid = num / jnp.maximum(jnp.abs(den), jnp.exp(-m_out))
        hid = jax.nn.sigmoid(mo_ref[:, hs].astype(F32)) * hid
        hid = hid * lax.rsqrt(jnp.mean(hid * hid, axis=-1, keepdims=True) + EPS)
        o_ref[:, hs] = (hid * nw_ref[:, hs]).astype(BF16)

        a_c = g_tot - b_c + ic_c
        m_new = jnp.maximum(g_tot + m_prev, jnp.max(a_c, axis=0, keepdims=True))
        decay = jnp.exp(g_tot + m_prev - m_new)
        wk = jnp.exp(a_c - m_new) * k
        c_ref[h] = decay * c_prev + lax.dot_general(
            wk.astype(BF16), v, (((0,), (0,)), ((), ())), preferred_element_type=F32)
        n_ref[h:h + 1, :] = decay * n_prev + jnp.sum(wk, axis=0, keepdims=True)
        m_ref[h:h + 1, :] = jnp.broadcast_to(m_new, (1, LANES))


def _mlstm(mqk, mv, mo, gif, conv_w, bias_row, norm_w):
    bsz, s, _ = mqk.shape
    L = MLSTM_CHUNK
    tok = lambda width: pl.BlockSpec((None, L, width), lambda b, c: (b, c, 0))
    full = lambda shape: pl.BlockSpec(shape, lambda b, c: (0, 0))
    return pl.pallas_call(
        _mlstm_kernel,
        grid=(bsz, s // L),
        in_specs=[tok(MQK_COLS), tok(MLSTM_W), tok(MLSTM_W), tok(IF_COLS),
                  full((CONV_WIDTH, MQK_COLS)), full((1, IF_COLS)), full((1, MLSTM_W))],
        out_specs=tok(MLSTM_W),
        out_shape=jax.ShapeDtypeStruct((bsz, s, MLSTM_W), BF16),
        scratch_shapes=[pltpu.VMEM((L + SUBLANES, MQK_COLS), F32),
                        pltpu.VMEM((MLSTM_HEADS, MLSTM_HEAD_DIM, MLSTM_HEAD_DIM), F32),
                        pltpu.VMEM((SUBLANES, MLSTM_HEAD_DIM), F32),
                        pltpu.VMEM((SUBLANES, LANES), F32)],
        compiler_params=_cparams(("arbitrary", "arbitrary")),
        name="mlstm",
    )(mqk, mv, mo, gif, conv_w, bias_row, norm_w.reshape(1, MLSTM_W))


def _merge_kernel(x_ref, att_ref, mls_ref, g_ref, gate_ref, wa_ref, wm_ref, wo_ref, h_ref):
    a = jnp.dot(att_ref[...], wa_ref[...], preferred_element_type=F32)
    m = jnp.dot(mls_ref[...], wm_ref[...], preferred_element_type=F32)
    ga = g_ref[:, :D_MODEL].astype(F32)
    gm = g_ref[:, D_MODEL:].astype(F32)
    merged = jax.nn.sigmoid(ga) * a + jax.nn.sigmoid(gm) * m
    y = jnp.dot(merged.astype(BF16), wo_ref[...], preferred_element_type=F32)
    h_ref[...] = x_ref[...] + gate_ref[...] * y


def _merge(x, att, mls, g, gate1, w_att, w_mls, w_out):
    bsz, s, d = x.shape
    tm = TOK_TILE
    tok = lambda width: pl.BlockSpec((None, tm, width), lambda b, i: (b, i, 0))
    full = lambda shape: pl.BlockSpec(shape, lambda b, i: (0, 0))
    return pl.pallas_call(
        _merge_kernel,
        grid=(bsz, s // tm),
        in_specs=[tok(d), tok(ATT_Q_W), tok(MLSTM_W), tok(GATE_COLS),
                  pl.BlockSpec((None, 1, d), lambda b, i: (b, 0, 0)),
                  full((ATT_Q_W, d)), full((MLSTM_W, d)), full((d, d))],
        out_specs=tok(d),
        out_shape=jax.ShapeDtypeStruct((bsz, s, d), F32),
        compiler_params=_cparams(("parallel", "parallel")),
        name="merge",
    )(x, att, mls, g, gate1, w_att, w_mls, w_out)


def _top16(s, iota_f):
    n_keys = s.shape[0]
    work = s
    rank = jnp.full(s.shape, float(PEER_TOPK), F32)
    vals = []
    for a in range(PEER_TOPK):
        m = jnp.max(work, axis=0, keepdims=True)
        idx = jnp.min(jnp.where(work == m, iota_f, float(n_keys)), axis=0, keepdims=True)
        sel = iota_f == idx
        rank = jnp.where(sel, float(a), rank)
        work = jnp.where(sel, NEG_INF, work)
        vals.append(m)
    return vals, rank


def _route_exact(s1, s2):
    k = PEER_TOPK
    nk = PEER_N_KEYS
    iota_keys = lax.broadcasted_iota(jnp.int32, (nk, LANES), 0).astype(F32)
    iota_k = lax.broadcasted_iota(jnp.int32, (k, LANES), 0).astype(F32)
    vals0, rank0 = _top16(s1, iota_keys)
    vals1, rank1 = _top16(s2, iota_keys)
    v0 = jnp.concatenate(vals0, axis=0)

    cnt = jnp.zeros((k, LANES), F32)
    front = v0 + vals1[0]
    c_max = vals0[0] + vals1[0]
    z = jnp.zeros((1, LANES), F32)
    for _ in range(k):
        m = jnp.max(front, axis=0, keepdims=True)
        idx = jnp.min(jnp.where(front == m, iota_k, float(k)), axis=0, keepdims=True)
        sel = iota_k == idx
        cnt = cnt + jnp.where(sel, 1.0, 0.0)
        z = z + jnp.exp(m - c_max)
        taken = jnp.sum(jnp.where(sel, cnt, 0.0), axis=0, keepdims=True)
        nxt = jnp.full((1, LANES), NEG_INF, F32)
        for b in range(1, k):
            nxt = jnp.where(taken == float(b), vals1[b], nxt)
        front = jnp.where(sel, v0 + nxt, front)

    cnt_keys = jnp.zeros((nk, LANES), F32)
    for a in range(k):
        cnt_keys = jnp.where(rank0 == float(a), cnt[a:a + 1, :], cnt_keys)
    return rank1, jnp.exp(s2 - vals1[0]), cnt_keys, jnp.exp(s1 - vals0[0]) / z


def _sort_network(n):
    def merge(lo, hi, r):
        step = r * 2
        if step < hi - lo:
            yield from merge(lo, hi, step)
            yield from merge(lo + r, hi, step)
            yield from [(i, i + r) for i in range(lo + r, hi - r, step)]
        else:
            yield (lo, lo + r)

    def sort(lo, hi):
        if hi - lo >= 1:
            mid = lo + (hi - lo) // 2
            yield from sort(lo, mid)
            yield from sort(mid + 1, hi)
            yield from merge(lo, hi, 1)

    return list(sort(0, n - 1))


def _top16_values(x):
    n = PEER_TOPK
    x = list(x) + [None] * (n - len(x))

    def top(a, b):
        return b if a is None else a if b is None else jnp.maximum(a, b)

    def cmpx(i, j):
        a, b = x[i], x[j]
        if b is None:
            return
        if a is None:
            x[i], x[j] = b, None
        else:
            x[i], x[j] = jnp.maximum(a, b), jnp.minimum(a, b)

    for i, j in _sort_network(n):
        cmpx(i, j)
    shift = SUBLANES // 2
    while shift >= 1:
        y = [None if t is None else pltpu.roll(t, shift, 0) for t in x]
        x = [top(x[i], y[n - 1 - i]) for i in range(n)]
        gap = n // 2
        while gap >= 1:
            for i in range(n):
                if i & gap == 0:
                    cmpx(i, i + gap)
            gap //= 2
        shift //= 2
    return x


def _route_fast(s1, s2):
    k = PEER_TOPK
    nk = PEER_N_KEYS
    tiles = nk // SUBLANES
    s1r = s1.reshape(tiles, SUBLANES, LANES)
    s2r = s2.reshape(tiles, SUBLANES, LANES)
    v0 = _top16_values([s1r[t] for t in range(tiles)])
    v1 = _top16_values([s2r[t] for t in range(tiles)])
    sub = lax.broadcasted_iota(jnp.int32, (SUBLANES, LANES), 0)
    one = jnp.ones((SUBLANES, LANES), F32)
    zero = jnp.zeros((SUBLANES, LANES), F32)

    tie = zero
    for v, sr in ((v0, s1r), (v1, s2r)):
        for a in range(k - 1):
            tie = jnp.where(v[a] == v[a + 1], one, tie)
        above = jnp.sum(jnp.where(sr >= v[k - 1], 1.0, 0.0), axis=0)
        tie = jnp.where(jnp.sum(above, axis=0, keepdims=True) != float(k), one, tie)

    def by_sublane(vals):
        out = vals[SUBLANES - 1]
        for a in range(SUBLANES - 2, -1, -1):
            out = jnp.where(sub == a, vals[a], out)
        return out

    v0_lo, v0_hi, v1_hi = by_sublane(v0[:SUBLANES]), by_sublane(v0[SUBLANES:]), by_sublane(v1[SUBLANES:])
    cand = [v0_lo + v1[0], v0_hi + v1[0]]
    for b in range(1, SUBLANES):
        cand.append(jnp.where(sub < (k // (b + 1)), v0_lo + v1[b], NEG_INF))
    cand.append(v0[0] + v1_hi)
    best = _top16_values(cand)
    z = one
    for a in range(1, k):
        z = z + jnp.exp(best[a] - best[0])
    picked = [jnp.where(t >= best[k - 1], one, zero) for t in cand]
    cnt_lo = picked[0]
    for t in picked[2:-1]:
        cnt_lo = cnt_lo + t
    cnt_lo = cnt_lo + jnp.where(sub == 0, jnp.sum(picked[-1], axis=0, keepdims=True), zero)
    cnt_hi = picked[1]
    total = jnp.sum(cnt_lo + cnt_hi, axis=0, keepdims=True)
    tie = jnp.where(total != float(k), one, tie)

    cnt_keys = jnp.zeros((tiles, SUBLANES, LANES), F32)
    for a in range(k):
        src = cnt_lo if a < SUBLANES else cnt_hi
        row = a % SUBLANES
        cnt_a = jnp.broadcast_to(src[row:row + 1, :], (SUBLANES, LANES))
        cnt_keys = jnp.where(s1r == v0[a], cnt_a, cnt_keys)
    rank1 = jnp.full((tiles, SUBLANES, LANES), float(k), F32)
    for a in range(k - 1, -1, -1):
        rank1 = jnp.where(v1[a] <= s2r, float(a), rank1)
    e2 = jnp.exp(s2r - v1[0])
    c1 = jnp.exp(s1r - v0[0]) / z
    flat = lambda t: t.reshape(nk, LANES)
    return flat(rank1), flat(e2), flat(cnt_keys), flat(c1), tie


def _fold_keys_kernel(keys_ref, wq_ref, o_ref):
    o_ref[...] = lax.dot_general(keys_ref[...], wq_ref[...], (((1,), (1,)), ((), ())),
                                 preferred_element_type=F32, precision=HIGHEST).astype(BF16)


def _fold_keys(keys, wq):
    d, qdim = wq.shape
    nk, half = keys.shape[1:]
    return pl.pallas_call(
        _fold_keys_kernel,
        grid=(qdim // half,),
        in_specs=[pl.BlockSpec((None, nk, half), lambda j: (j % 2, 0, 0)),
                  pl.BlockSpec((d, half), lambda j: (0, j))],
        out_specs=pl.BlockSpec((nk, d), lambda j: (j, 0)),
        out_shape=jax.ShapeDtypeStruct((qdim // half * nk, d), BF16),
        compiler_params=_cparams(("arbitrary",)),
        name="fold_keys",
    )(keys, wq)


def _route_kernel(h_ref, nw_ref, sc_ref, sh_ref, ws_ref,
                  r2_ref, e2_ref, cnt_ref, c1_ref, score_ref):
    u2 = (_rmsnorm_rows(h_ref[...], nw_ref[...]) * (1.0 + sc_ref[...]) + sh_ref[...]).astype(BF16)
    score_ref[...] = lax.dot_general(ws_ref[...], u2, (((1,), (1,)), ((), ())),
                                     preferred_element_type=F32)
    n_col = h_ref.shape[0] // LANES
    nk = PEER_N_KEYS

    def body(it, carry):
        head = it // n_col
        tcol = it % n_col
        lanes = pl.ds(pl.multiple_of(tcol * LANES, LANES), LANES)
        rows0 = pl.ds(pl.multiple_of(head * (2 * nk), nk), nk)
        rows1 = pl.ds(pl.multiple_of(head * (2 * nk) + nk, nk), nk)
        s1 = score_ref[rows0, lanes]
        s2 = score_ref[rows1, lanes]

        def store(rank1, e2, cnt_keys, c1):
            r2_ref[head, tcol] = pltpu.bitcast(rank1.astype(BF16), jnp.int32)
            e2_ref[head, tcol] = pltpu.bitcast(e2.astype(BF16), jnp.int32)
            cnt_ref[head, tcol] = cnt_keys
            c1_ref[head, tcol] = c1

        rank1, e2, cnt_keys, c1, tie = _route_fast(s1, s2)
        store(rank1, e2, cnt_keys, c1)

        @pl.when(jnp.max(tie) > 0.0)
        def _():
            store(*_route_exact(score_ref[rows0, lanes], score_ref[rows1, lanes]))

        return carry

    lax.fori_loop(0, PEER_HEADS * n_col, body, 0)


def _route(h, norm_w, sc, sh, w_score):
    bsz, s, d = h.shape
    tt = PEER_TOK
    sel = lambda rows: pl.BlockSpec((None, PEER_HEADS, tt // LANES, rows, LANES), lambda b, t: (b, 0, t, 0, 0))
    out = lambda rows, dt: jax.ShapeDtypeStruct((bsz, PEER_HEADS, s // LANES, rows, LANES), dt)
    packed = PEER_N_KEYS * 2 // 4
    n_scores = w_score.shape[0]
    return pl.pallas_call(
        _route_kernel,
        grid=(bsz, s // tt),
        in_specs=[pl.BlockSpec((None, tt, d), lambda b, t: (b, t, 0)),
                  pl.BlockSpec((1, d), lambda b, t: (0, 0)),
                  pl.BlockSpec((None, 1, d), lambda b, t: (b, 0, 0)),
                  pl.BlockSpec((None, 1, d), lambda b, t: (b, 0, 0)),
                  pl.BlockSpec((n_scores, d), lambda b, t: (0, 0))],
        out_specs=[sel(packed), sel(packed), sel(PEER_N_KEYS), sel(PEER_N_KEYS)],
        out_shape=[out(packed, jnp.int32), out(packed, jnp.int32),
                   out(PEER_N_KEYS, F32), out(PEER_N_KEYS, F32)],
        scratch_shapes=[pltpu.VMEM((n_scores, tt), F32)],
        compiler_params=_cparams(("parallel", "parallel")),
        name="route",
    )(h, norm_w.reshape(1, d), sc, sh, w_score)


def _peer_kernel(h_ref, nw_ref, sc_ref, sh_ref, gate_ref, nf_ref,
                 u_first_ref, u_next_ref, vt_prev_ref, vt_last_ref,
                 r2_ref, e2_ref, cnt_ref, c1_ref, o_ref,
                 x_ref, at_even_ref, at_odd_ref, wg_even_ref, wg_odd_ref, acc_ref,
                 *, final_norm, n_blocks):
    c = pl.program_id(2)
    tt, d = h_ref.shape
    nk = PEER_N_KEYS
    ncol = tt // LANES
    gk = PEER_GROUP_KEYS
    dq = d // PEER_TRIPS
    eq = PEER_IBLK * nk // PEER_TRIPS

    def pre_activations(u_packed, at_ref, rows):
        pre = lax.dot_general(pltpu.bitcast(u_packed, BF16), x_ref[...], (((1,), (1,)), ((), ())),
                              preferred_element_type=F32)
        for tcol in range(ncol):
            at_ref[tcol, rows, :] = pre[:, tcol * LANES:(tcol + 1) * LANES]

    @pl.when(c == 0)
    def _():
        u2 = _rmsnorm_rows(h_ref[...], nw_ref[...]) * (1.0 + sc_ref[...]) + sh_ref[...]
        x_ref[...] = u2.astype(BF16)
        acc_ref[...] = jnp.zeros_like(acc_ref)
        wg_odd_ref[...] = jnp.zeros(wg_odd_ref.shape, BF16)
        pre_activations(u_first_ref[...], at_even_ref, slice(None))

    def row_tile(ref, hd, tcol, key):
        return jnp.broadcast_to(ref[hd, tcol, key:key + 1, :], (nk, LANES)).astype(BF16)

    def gates(wg_ref):
        return jnp.concatenate([wg_ref[tcol] for tcol in range(ncol)], axis=1)

    def step(at_ref, at_next_ref, wg_ref, wg_prev_ref):
        def body(q, carry):
            for j in range(ncol // PEER_TRIPS):
                tcol = q * (ncol // PEER_TRIPS) + j
                for k0 in range(0, PEER_IBLK, gk):
                    w = [jnp.zeros((nk, LANES), BF16) for _ in range(gk)]
                    for hd in range(PEER_HEADS):
                        r2 = pltpu.bitcast(r2_ref[hd, tcol], BF16)
                        e2 = pltpu.bitcast(e2_ref[hd, tcol], BF16)
                        for k in range(gk):
                            sel = r2 < row_tile(cnt_ref, hd, tcol, k0 + k)
                            w[k] = w[k] + (jnp.where(sel, e2, jnp.zeros_like(e2))
                                           * row_tile(c1_ref, hd, tcol, k0 + k))
                    for k in range(gk):
                        rows = slice((k0 + k) * nk, (k0 + k + 1) * nk)
                        a = at_ref[tcol, rows, :]
                        act = 0.5 * a * (1.0 + lax.erf(a * np.float32(math.sqrt(0.5))))
                        wg_ref[tcol, rows, :] = w[k] * act.astype(BF16)
            out_rows = pl.ds(pl.multiple_of(q * dq, dq), dq)
            vt_rows = pl.ds(pl.multiple_of(q * (dq // 2), dq // 2), dq // 2)
            acc_ref[out_rows, :] += jnp.dot(pltpu.bitcast(vt_prev_ref[vt_rows, :], BF16), gates(wg_prev_ref),
                                            preferred_element_type=F32)
            u_rows = pl.ds(pl.multiple_of(q * (eq // 2), eq // 2), eq // 2)
            pre_activations(u_next_ref[u_rows, :], at_next_ref, pl.ds(pl.multiple_of(q * eq, eq), eq))
            return carry

        lax.fori_loop(0, PEER_TRIPS, body, 0)

    @pl.when(c % 2 == 0)
    def _():
        step(at_even_ref, at_odd_ref, wg_even_ref, wg_odd_ref)

    @pl.when(c % 2 == 1)
    def _():
        step(at_odd_ref, at_even_ref, wg_odd_ref, wg_even_ref)

    @pl.when(c == n_blocks - 1)
    def _():
        wg_last_ref = wg_odd_ref if (n_blocks - 1) % 2 else wg_even_ref
        total = acc_ref[...] + jnp.dot(pltpu.bitcast(vt_last_ref[...], BF16), gates(wg_last_ref),
                                       preferred_element_type=F32)
        hh = h_ref[...] + gate_ref[...] * total.T
        o_ref[...] = _rmsnorm_rows(hh, nf_ref[...]) if final_norm else hh


def _peer(h, norm_w, sc, sh, gate2, normf_w, u_bf, vt_bf, r2, e2, cnt, c1, final_norm):
    bsz, s, d = h.shape
    tt = PEER_TOK
    ec = PEER_IBLK * PEER_N_KEYS
    n_exp = 2 * u_bf.shape[0]
    vec = pl.BlockSpec((1, d), lambda b, t, c: (0, 0))
    bvec = pl.BlockSpec((None, 1, d), lambda b, t, c: (b, 0, 0))
    ncol = tt // LANES
    selj = pl.BlockSpec((None, PEER_HEADS, ncol, PEER_N_KEYS // 2, LANES), lambda b, t, c: (b, 0, t, 0, 0))
    seli = pl.BlockSpec((None, PEER_HEADS, ncol, PEER_IBLK, LANES), lambda b, t, c: (b, 0, t, c, 0))
    n_blocks = n_exp // ec
    return pl.pallas_call(
        functools.partial(_peer_kernel, final_norm=final_norm, n_blocks=n_blocks),
        grid=(bsz, s // tt, n_blocks),
        in_specs=[pl.BlockSpec((None, tt, d), lambda b, t, c: (b, t, 0)),
                  vec, bvec, bvec, bvec, vec,
                  pl.BlockSpec((ec // 2, d), lambda b, t, c: (0, 0)),
                  pl.BlockSpec((ec // 2, d), lambda b, t, c: (jnp.minimum(c + 1, n_blocks - 1), 0)),
                  pl.BlockSpec((d // 2, ec), lambda b, t, c: (0, jnp.maximum(c - 1, 0))),
                  pl.BlockSpec((d // 2, ec), lambda b, t, c: (0, n_blocks - 1)),
                  selj, selj, seli, seli],
        out_specs=pl.BlockSpec((None, tt, d), lambda b, t, c: (b, t, 0)),
        out_shape=jax.ShapeDtypeStruct((bsz, s, d), F32),
        scratch_shapes=[pltpu.VMEM((tt, d), BF16),
                        pltpu.VMEM((ncol, ec, LANES), F32),
                        pltpu.VMEM((ncol, ec, LANES), F32),
                        pltpu.VMEM((ncol, ec, LANES), BF16),
                        pltpu.VMEM((ncol, ec, LANES), BF16),
                        pltpu.VMEM((d, tt), F32)],
        compiler_params=_cparams(("parallel", "parallel", "arbitrary")),
        name="peer",
    )(h, norm_w.reshape(1, d), sc, sh, gate2, normf_w.reshape(1, d), u_bf, u_bf, vt_bf, vt_bf,
      r2, e2, cnt, c1)


def _rope_tables(s):
    half = ATT_HEAD_DIM // 2
    inv_freq = ROPE_THETA ** (-jnp.arange(half, dtype=F32) * 2.0 / ATT_HEAD_DIM)
    ang = jnp.arange(s, dtype=F32)[:, None] * inv_freq[None, :]
    cos, sin = jnp.cos(ang), jnp.sin(ang)
    reps = LANES // ATT_HEAD_DIM
    cos_t = jnp.tile(jnp.concatenate([cos, cos], axis=-1), (1, reps))
    sin_t = jnp.tile(jnp.concatenate([-sin, sin], axis=-1), (1, reps))
    return cos_t, sin_t


def _pack_rows(x):
    rows, cols = x.shape
    pairs = x.reshape(rows // 2, 2, cols).transpose(0, 2, 1)
    return lax.bitcast_convert_type(pairs, jnp.int32)


def _permute_in_proj(w_in):
    n_if = 2 * MLSTM_HEADS
    lo = ATT_COLS + MQK_COLS + 2 * MLSTM_W
    pad = jnp.zeros((w_in.shape[0], IF_COLS - n_if), w_in.dtype)
    return jnp.concatenate([w_in[:, :lo], w_in[:, lo + n_if:], w_in[:, lo:lo + n_if], pad], axis=1)


def kernel(x, c, w_ada, b_ada, norm1_w, w_in, conv_w, att_sinks, i_bias, f_bias, mlstm_norm_w,
           w_att_branch, w_mlstm_branch, w_out, norm2_w, peer_w_query, peer_sub_keys, peer_u,
           peer_v, norm_f_w):
    depth = w_ada.shape[0]
    bsz, s, d = x.shape
    cos_t, sin_t = _rope_tables(s)
    h = x
    for l in range(depth):
        mod = _ada(c, w_ada[l], b_ada[l]).reshape(bsz, 6, 1, d)
        sh1, sc1, g1, sh2, sc2, g2 = (mod[:, i] for i in range(6))

        w_perm = _permute_in_proj(w_in[l]).astype(BF16)
        att_in, mqk, mv, mo, gates, gif = _inproj(h, norm1_w[l], sc1, sh1, w_perm)
        att = _attention(att_in, att_sinks[l], cos_t, sin_t)
        bias_row = jnp.concatenate(
            [i_bias[l], f_bias[l], jnp.zeros((IF_COLS - 2 * MLSTM_HEADS,), F32)]).reshape(1, IF_COLS)
        mls = _mlstm(mqk, mv, mo, gif, conv_w[l], bias_row, mlstm_norm_w[l])
        h = _merge(h, att, mls, gates, g1, w_att_branch[l].astype(BF16),
                   w_mlstm_branch[l].astype(BF16), w_out[l].astype(BF16))

        w_score = _fold_keys(peer_sub_keys[l], peer_w_query[l])
        r2, e2, cnt, c1 = _route(h, norm2_w[l], sc2, sh2, w_score)
        h = _peer(h, norm2_w[l], sc2, sh2, g2, norm_f_w, _pack_rows(peer_u[l].astype(BF16)),
                  _pack_rows(peer_v[l].T.astype(BF16)), r2, e2, cnt, c1, final_norm=(l == depth - 1))
    return h
```

```python
import functools
import math

import numpy as np
import jax
import jax.numpy as jnp
from jax import lax
from jax.experimental import pallas as pl
from jax.experimental.pallas import tpu as pltpu

F32 = jnp.float32
BF16 = jnp.bfloat16
HIGHEST = lax.Precision.HIGHEST

D_MODEL = 1024
ATT_HEADS = 8
ATT_KV_HEADS = 2
ATT_HEAD_DIM = 64
ATT_BLOCK = 128
ROPE_THETA = 10000.0
MLSTM_HEADS = 4
MLSTM_HEAD_DIM = 128
MLSTM_CHUNK = 128
CONV_WIDTH = 4
ATT_Q_W = ATT_HEADS * ATT_HEAD_DIM
ATT_KV_W = ATT_KV_HEADS * ATT_HEAD_DIM
MLSTM_W = MLSTM_HEADS * MLSTM_HEAD_DIM
PEER_HEADS = 8
PEER_N_KEYS = 128
PEER_HALF = 128
PEER_TOPK = 16
EPS = 1e-6

LANES = 128
SUBLANES = 8
BF16_ROWS = 16
VMEM_LIMIT = 48 * 1024 * 1024

ATT_COLS = ATT_Q_W + 2 * ATT_KV_W
MQK_COLS = 2 * MLSTM_W
GATE_COLS = 2 * D_MODEL
IF_COLS = LANES
OFF_ATT = 0
OFF_MQK = OFF_ATT + ATT_COLS
OFF_MV = OFF_MQK + MQK_COLS
OFF_MO = OFF_MV + MLSTM_W
OFF_G = OFF_MO + MLSTM_W
OFF_IF = OFF_G + GATE_COLS
IN_COLS = OFF_IF + IF_COLS

TOK_TILE = 512
PEER_TOK = 512
PEER_IBLK = 8
PEER_TRIPS = 2
PEER_GROUP_KEYS = 1
NEG_INF = float("-inf")


def _cparams(sem):
    return pltpu.CompilerParams(dimension_semantics=sem, vmem_limit_bytes=VMEM_LIMIT)


def _rmsnorm_rows(x, w):
    return x * lax.rsqrt(jnp.mean(x * x, axis=-1, keepdims=True) + EPS) * w


def _ada_kernel(c_ref, w_ref, b_ref, o_ref):
    c = c_ref[...]
    act = c * jax.nn.sigmoid(c)
    o_ref[...] = jnp.dot(act, w_ref[...], preferred_element_type=F32, precision=HIGHEST) + b_ref[...]


def _ada(c, w, b):
    bsz, d = c.shape
    n_out = w.shape[1]
    return pl.pallas_call(
        _ada_kernel,
        grid=(n_out // d,),
        in_specs=[pl.BlockSpec((bsz, d), lambda j: (0, 0)),
                  pl.BlockSpec((d, d), lambda j: (0, j)),
                  pl.BlockSpec((1, d), lambda j: (0, j))],
        out_specs=pl.BlockSpec((bsz, d), lambda j: (0, j)),
        out_shape=jax.ShapeDtypeStruct((bsz, n_out), F32),
        compiler_params=_cparams(("arbitrary",)),
        name="ada",
    )(c, w, b.reshape(1, n_out))


def _inproj_kernel(x_ref, nw_ref, sc_ref, sh_ref, w_ref,
                   att_ref, mqk_ref, mv_ref, mo_ref, g_ref, if_ref):
    y = _rmsnorm_rows(x_ref[...], nw_ref[...])
    u = (y * (1.0 + sc_ref[...]) + sh_ref[...]).astype(BF16)

    def proj(lo, width):
        return jnp.dot(u, w_ref[:, lo:lo + width], preferred_element_type=F32)

    att_ref[...] = proj(OFF_ATT, ATT_COLS).astype(BF16)
    mqk_ref[...] = proj(OFF_MQK, MQK_COLS).astype(BF16)
    mv_ref[...] = proj(OFF_MV, MLSTM_W).astype(BF16)
    mo_ref[...] = proj(OFF_MO, MLSTM_W).astype(BF16)
    g_ref[...] = proj(OFF_G, GATE_COLS).astype(BF16)
    if_ref[...] = proj(OFF_IF, IF_COLS)


def _inproj(x, norm_w, sc, sh, w_perm):
    bsz, s, d = x.shape
    tm = TOK_TILE
    tok = lambda width: pl.BlockSpec((None, tm, width), lambda b, i: (b, i, 0))
    vec = pl.BlockSpec((None, 1, d), lambda b, i: (b, 0, 0))
    widths = (ATT_COLS, MQK_COLS, MLSTM_W, MLSTM_W, GATE_COLS, IF_COLS)
    dtypes = (BF16, BF16, BF16, BF16, BF16, F32)
    return pl.pallas_call(
        _inproj_kernel,
        grid=(bsz, s // tm),
        in_specs=[tok(d),
                  pl.BlockSpec((1, d), lambda b, i: (0, 0)),
                  vec, vec,
                  pl.BlockSpec((d, IN_COLS), lambda b, i: (0, 0))],
        out_specs=[tok(w) for w in widths],
        out_shape=[jax.ShapeDtypeStruct((bsz, s, w), dt) for w, dt in zip(widths, dtypes)],
        compiler_params=_cparams(("parallel", "parallel")),
        name="inproj",
    )(x, norm_w.reshape(1, d), sc, sh, w_perm)


def _attn_kernel(sink_ref, cur_ref, prev_ref, cosc_ref, sinc_ref, cosp_ref, sinp_ref, o_ref):
    blk = ATT_BLOCK
    n = pl.program_id(1)
    lane = lax.broadcasted_iota(jnp.int32, (blk, LANES), 1)
    first_half = (lane & (ATT_HEAD_DIM - 1)) < (ATT_HEAD_DIM // 2)
    low = lane < ATT_HEAD_DIM

    def rope(xf, cos, sin):
        rot = jnp.where(first_half, pltpu.roll(xf, LANES - ATT_HEAD_DIM // 2, 1),
                        pltpu.roll(xf, ATT_HEAD_DIM // 2, 1))
        return xf * cos + rot * sin

    def dup(xf, g):
        sw = pltpu.roll(xf, ATT_HEAD_DIM, 1)
        return jnp.where(low, xf, sw) if g == 0 else jnp.where(low, sw, xf)

    cosc, sinc = cosc_ref[...], sinc_ref[...]
    cosp, sinp = cosp_ref[...], sinp_ref[...]
    k_cur = rope(cur_ref[:, ATT_Q_W:ATT_Q_W + ATT_KV_W].astype(F32), cosc, sinc)
    k_prev = rope(prev_ref[:, ATT_Q_W:ATT_Q_W + ATT_KV_W].astype(F32), cosp, sinp)
    v_cur = cur_ref[:, ATT_Q_W + ATT_KV_W:ATT_COLS].astype(F32)
    v_prev = prev_ref[:, ATT_Q_W + ATT_KV_W:ATT_COLS].astype(F32)

    qi = lax.broadcasted_iota(jnp.int32, (blk, 2 * blk), 0)
    ki = lax.broadcasted_iota(jnp.int32, (blk, 2 * blk), 1)
    q_lim = jnp.where(n > 0, qi, 2 * blk)
    mask = ((ki < blk) & (ki > q_lim)) | ((ki >= blk) & ((ki - blk) <= qi))

    scale = ATT_HEAD_DIM ** -0.5
    group = ATT_HEADS // ATT_KV_HEADS
    for g in range(ATT_KV_HEADS):
        kk = jnp.concatenate([dup(k_prev, g), dup(k_cur, g)], axis=0).astype(BF16)
        vv = jnp.concatenate([dup(v_prev, g), dup(v_cur, g)], axis=0).astype(BF16)
        for pair in range(group // 2):
            j = g * (group // 2) + pair
            qp = rope(cur_ref[:, j * LANES:(j + 1) * LANES].astype(F32), cosc, sinc) * scale
            outs = []
            for half in range(2):
                head = 2 * j + half
                qh = jnp.where(low, qp, 0.0) if half == 0 else jnp.where(low, 0.0, qp)
                s = lax.dot_general(qh.astype(BF16), kk, (((1,), (1,)), ((), ())),
                                    preferred_element_type=F32)
                s = jnp.where(mask, s, NEG_INF)
                sink = sink_ref[head]
                m = jnp.maximum(jnp.max(s, axis=-1, keepdims=True), sink)
                p = jnp.exp(s - m)
                denom = jnp.sum(p, axis=-1, keepdims=True) + jnp.exp(sink - m)
                o = jnp.dot(p.astype(BF16), vv, preferred_element_type=F32)
                outs.append(o / denom)
            o_ref[:, j * LANES:(j + 1) * LANES] = jnp.where(low, outs[0], outs[1]).astype(BF16)


def _attention(att_in, sinks, cos_t, sin_t):
    bsz, s, _ = att_in.shape
    blk = ATT_BLOCK
    nb = s // blk
    prev = lambda n: jnp.maximum(n - 1, 0)
    return pl.pallas_call(
        _attn_kernel,
        grid=(bsz, nb),
        in_specs=[pl.BlockSpec(memory_space=pltpu.SMEM),
                  pl.BlockSpec((None, blk, ATT_COLS), lambda b, n: (b, n, 0)),
                  pl.BlockSpec((None, blk, ATT_COLS), lambda b, n: (b, prev(n), 0)),
                  pl.BlockSpec((blk, LANES), lambda b, n: (n, 0)),
                  pl.BlockSpec((blk, LANES), lambda b, n: (n, 0)),
                  pl.BlockSpec((blk, LANES), lambda b, n: (prev(n), 0)),
                  pl.BlockSpec((blk, LANES), lambda b, n: (prev(n), 0))],
        out_specs=pl.BlockSpec((None, blk, ATT_Q_W), lambda b, n: (b, n, 0)),
        out_shape=jax.ShapeDtypeStruct((bsz, s, ATT_Q_W), BF16),
        compiler_params=_cparams(("parallel", "parallel")),
        name="attn",
    )(sinks, att_in, att_in, cos_t, sin_t, cos_t, sin_t)


def _mlstm_kernel(mqk_ref, mv_ref, mo_ref, if_ref, convw_ref, bias_ref, nw_ref,
                  o_ref, buf_ref, c_ref, n_ref, m_ref):
    L = MLSTM_CHUNK
    d = MLSTM_HEAD_DIM
    tail = SUBLANES

    @pl.when(pl.program_id(1) == 0)
    def _():
        buf_ref[0:tail, :] = jnp.zeros((tail, 2 * MLSTM_W), F32)
        c_ref[...] = jnp.zeros_like(c_ref)
        n_ref[...] = jnp.zeros_like(n_ref)
        m_ref[...] = jnp.zeros_like(m_ref)

    buf_ref[tail:tail + L, :] = mqk_ref[...].astype(F32)
    y = jnp.zeros((L, 2 * MLSTM_W), F32)
    for j in range(CONV_WIDTH):
        off = tail - (CONV_WIDTH - 1) + j
        y = y + convw_ref[j:j + 1, :] * buf_ref[off:off + L, :]
    buf_ref[0:tail, :] = buf_ref[L:L + tail, :]
    y = y * jax.nn.sigmoid(y)
    q_all = y[:, :MLSTM_W]
    k_all = y[:, MLSTM_W:] * (d ** -0.5)

    gates = if_ref[...] + bias_ref[...]
    lf = jnp.minimum(gates, 0.0) - jnp.log1p(jnp.exp(-jnp.abs(gates)))
    row = lax.broadcasted_iota(jnp.int32, (L, L), 0)
    col = lax.broadcasted_iota(jnp.int32, (L, L), 1)
    causal = col <= row
    tril = causal.astype(F32)
    triu = (row <= col).astype(F32)
    bcum_c = jnp.dot(tril, lf, preferred_element_type=F32, precision=HIGHEST)
    gates_t = gates.T
    bcum_r = jnp.dot(lf.T, triu, preferred_element_type=F32, precision=HIGHEST)

    for h in range(MLSTM_HEADS):
        hs = slice(h * d, (h + 1) * d)
        fl = MLSTM_HEADS + h
        q = q_all[:, hs]
        k = k_all[:, hs]
        v = mv_ref[:, hs]
        ic_c = gates[:, h:h + 1]
        ic_r = gates_t[h:h + 1, :]
        b_c = bcum_c[:, fl:fl + 1]
        b_r = bcum_r[fl:fl + 1, :]
        g_tot = b_c[L - 1:L, :]
        c_prev = c_ref[h]
        n_prev = n_ref[h:h + 1, :]
        m_prev = m_ref[h:h + 1, 0:1]

        dmat = jnp.where(causal, b_c - b_r + ic_r, NEG_INF)
        m_inter = b_c + m_prev
        m_out = jnp.maximum(m_inter, jnp.max(dmat, axis=-1, keepdims=True))
        qb = q.astype(BF16)
        kb = k.astype(BF16)
        qk = lax.dot_general(qb, kb, (((1,), (1,)), ((), ())), preferred_element_type=F32)
        w_intra = jnp.exp(dmat - m_out) * qk
        w_inter = jnp.exp(m_inter - m_out)
        num = (jnp.dot(w_intra.astype(BF16), v, preferred_element_type=F32)
               + w_inter * jnp.dot(qb, c_prev.astype(BF16), preferred_element_type=F32))
        den = (jnp.sum(w_intra, axis=-1, keepdims=True)
               + w_inter * jnp.sum(q * n_prev, axis=-1, keepdims=True))
        hid = num / jnp.maximum(jnp.abs(den), jnp.exp(-m_out))
        hid = jax.nn.sigmoid(mo_ref[:, hs].astype(F32)) * hid
        hid = hid * lax.rsqrt(jnp.mean(hid * hid, axis=-1, keepdims=True) + EPS)
        o_ref[:, hs] = (hid * nw_ref[:, hs]).astype(BF16)

        a_c = g_tot - b_c + ic_c
        m_new = jnp.maximum(g_tot + m_prev, jnp.max(a_c, axis=0, keepdims=True))
        decay = jnp.exp(g_tot + m_prev - m_new)
        wk = jnp.exp(a_c - m_new) * k
        c_ref[h] = decay * c_prev + lax.dot_general(
            wk.astype(BF16), v, (((0,), (0,)), ((), ())), preferred_element_type=F32)
        n_ref[h:h + 1, :] = decay * n_prev + jnp.sum(wk, axis=0, keepdims=True)
        m_ref[h:h + 1, :] = jnp.broadcast_to(m_new, (1, LANES))


def _mlstm(mqk, mv, mo, gif, conv_w, bias_row, norm_w):
    bsz, s, _ = mqk.shape
    L = MLSTM_CHUNK
    tok = lambda width: pl.BlockSpec((None, L, width), lambda b, c: (b, c, 0))
    full = lambda shape: pl.BlockSpec(shape, lambda b, c: (0, 0))
    return pl.pallas_call(
        _mlstm_kernel,
        grid=(bsz, s // L),
        in_specs=[tok(MQK_COLS), tok(MLSTM_W), tok(MLSTM_W), tok(IF_COLS),
                  full((CONV_WIDTH, MQK_COLS)), full((1, IF_COLS)), full((1, MLSTM_W))],
        out_specs=tok(MLSTM_W),
        out_shape=jax.ShapeDtypeStruct((bsz, s, MLSTM_W), BF16),
        scratch_shapes=[pltpu.VMEM((L + SUBLANES, MQK_COLS), F32),
                        pltpu.VMEM((MLSTM_HEADS, MLSTM_HEAD_DIM, MLSTM_HEAD_DIM), F32),
                        pltpu.VMEM((SUBLANES, MLSTM_HEAD_DIM), F32),
                        pltpu.VMEM((SUBLANES, LANES), F32)],
        compiler_params=_cparams(("arbitrary", "arbitrary")),
        name="mlstm",
    )(mqk, mv, mo, gif, conv_w, bias_row, norm_w.reshape(1, MLSTM_W))


def _merge_kernel(x_ref, att_ref, mls_ref, g_ref, gate_ref, wa_ref, wm_ref, wo_ref, h_ref):
    a = jnp.dot(att_ref[...], wa_ref[...], preferred_element_type=F32)
    m = jnp.dot(mls_ref[...], wm_ref[...], preferred_element_type=F32)
    ga = g_ref[:, :D_MODEL].astype(F32)
    gm = g_ref[:, D_MODEL:].astype(F32)
    merged = jax.nn.sigmoid(ga) * a + jax.nn.sigmoid(gm) * m
    y = jnp.dot(merged.astype(BF16), wo_ref[...], preferred_element_type=F32)
    h_ref[...] = x_ref[...] + gate_ref[...] * y


def _merge(x, att, mls, g, gate1, w_att, w_mls, w_out):
    bsz, s, d = x.shape
    tm = TOK_TILE
    tok = lambda width: pl.BlockSpec((None, tm, width), lambda b, i: (b, i, 0))
    full = lambda shape: pl.BlockSpec(shape, lambda b, i: (0, 0))
    return pl.pallas_call(
        _merge_kernel,
        grid=(bsz, s // tm),
        in_specs=[tok(d), tok(ATT_Q_W), tok(MLSTM_W), tok(GATE_COLS),
                  pl.BlockSpec((None, 1, d), lambda b, i: (b, 0, 0)),
                  full((ATT_Q_W, d)), full((MLSTM_W, d)), full((d, d))],
        out_specs=tok(d),
        out_shape=jax.ShapeDtypeStruct((bsz, s, d), F32),
        compiler_params=_cparams(("parallel", "parallel")),
        name="merge",
    )(x, att, mls, g, gate1, w_att, w_mls, w_out)


def _top16(s, iota_f):
    n_keys = s.shape[0]
    work = s
    rank = jnp.full(s.shape, float(PEER_TOPK), F32)
    vals = []
    for a in range(PEER_TOPK):
        m = jnp.max(work, axis=0, keepdims=True)
        idx = jnp.min(jnp.where(work == m, iota_f, float(n_keys)), axis=0, keepdims=True)
        sel = iota_f == idx
        rank = jnp.where(sel, float(a), rank)
        work = jnp.where(sel, NEG_INF, work)
        vals.append(m)
    return vals, rank


def _route_exact(s1, s2):
    k = PEER_TOPK
    nk = PEER_N_KEYS
    iota_keys = lax.broadcasted_iota(jnp.int32, (nk, LANES), 0).astype(F32)
    iota_k = lax.broadcasted_iota(jnp.int32, (k, LANES), 0).astype(F32)
    vals0, rank0 = _top16(s1, iota_keys)
    vals1, rank1 = _top16(s2, iota_keys)
    v0 = jnp.concatenate(vals0, axis=0)

    cnt = jnp.zeros((k, LANES), F32)
    front = v0 + vals1[0]
    c_max = vals0[0] + vals1[0]
    z = jnp.zeros((1, LANES), F32)
    for _ in range(k):
        m = jnp.max(front, axis=0, keepdims=True)
        idx = jnp.min(jnp.where(front == m, iota_k, float(k)), axis=0, keepdims=True)
        sel = iota_k == idx
        cnt = cnt + jnp.where(sel, 1.0, 0.0)
        z = z + jnp.exp(m - c_max)
        taken = jnp.sum(jnp.where(sel, cnt, 0.0), axis=0, keepdims=True)
        nxt = jnp.full((1, LANES), NEG_INF, F32)
        for b in range(1, k):
            nxt = jnp.where(taken == float(b), vals1[b], nxt)
        front = jnp.where(sel, v0 + nxt, front)

    cnt_keys = jnp.zeros((nk, LANES), F32)
    for a in range(k):
        cnt_keys = jnp.where(rank0 == float(a), cnt[a:a + 1, :], cnt_keys)
    return rank1, jnp.exp(s2 - vals1[0]), cnt_keys, jnp.exp(s1 - vals0[0]) / z


def _sort_network(n):
    def merge(lo, hi, r):
        step = r * 2
        if step < hi - lo:
            yield from merge(lo, hi, step)
            yield from merge(lo + r, hi, step)
            yield from [(i, i + r) for i in range(lo + r, hi - r, step)]
        else:
            yield (lo, lo + r)

    def sort(lo, hi):
        if hi - lo >= 1:
            mid = lo + (hi - lo) // 2
            yield from sort(lo, mid)
            yield from sort(mid + 1, hi)
            yield from merge(lo, hi, 1)

    return list(sort(0, n - 1))


def _top16_values(x):
    n = PEER_TOPK
    x = list(x) + [None] * (n - len(x))

    def top(a, b):
        return b if a is None else a if b is None else jnp.maximum(a, b)

    def cmpx(i, j):
        a, b = x[i], x[j]
        if b is None:
            return
        if a is None:
            x[i], x[j] = b, None
        else:
            x[i], x[j] = jnp.maximum(a, b), jnp.minimum(a, b)

    for i, j in _sort_network(n):
        cmpx(i, j)
    shift = SUBLANES // 2
    while shift >= 1:
        y = [None if t is None else pltpu.roll(t, shift, 0) for t in x]
        x = [top(x[i], y[n - 1 - i]) for i in range(n)]
        gap = n // 2
        while gap >= 1:
            for i in range(n):
                if i & gap == 0:
                    cmpx(i, i + gap)
            gap //= 2
        shift //= 2
    return x


def _route_fast(s1, s2):
    k = PEER_TOPK
    nk = PEER_N_KEYS
    tiles = nk // SUBLANES
    s1r = s1.reshape(tiles, SUBLANES, LANES)
    s2r = s2.reshape(tiles, SUBLANES, LANES)
    v0 = _top16_values([s1r[t] for t in range(tiles)])
    v1 = _top16_values([s2r[t] for t in range(tiles)])
    sub = lax.broadcasted_iota(jnp.int32, (SUBLANES, LANES), 0)
    one = jnp.ones((SUBLANES, LANES), F32)
    zero = jnp.zeros((SUBLANES, LANES), F32)

    tie = zero
    for v, sr in ((v0, s1r), (v1, s2r)):
        for a in range(k - 1):
            tie = jnp.where(v[a] == v[a + 1], one, tie)
        above = jnp.sum(jnp.where(sr >= v[k - 1], 1.0, 0.0), axis=0)
        tie = jnp.where(jnp.sum(above, axis=0, keepdims=True) != float(k), one, tie)

    def by_sublane(vals):
        out = vals[SUBLANES - 1]
        for a in range(SUBLANES - 2, -1, -1):
            out = jnp.where(sub == a, vals[a], out)
        return out

    v0_lo, v0_hi, v1_hi = by_sublane(v0[:SUBLANES]), by_sublane(v0[SUBLANES:]), by_sublane(v1[SUBLANES:])
    cand = [v0_lo + v1[0], v0_hi + v1[0]]
    for b in range(1, SUBLANES):
        cand.append(jnp.where(sub < (k // (b + 1)), v0_lo + v1[b], NEG_INF))
    cand.append(v0[0] + v1_hi)
    best = _top16_values(cand)
    z = one
    for a in range(1, k):
        z = z + jnp.exp(best[a] - best[0])
    picked = [jnp.where(t >= best[k - 1], one, zero) for t in cand]
    cnt_lo = picked[0]
    for t in picked[2:-1]:
        cnt_lo = cnt_lo + t
    cnt_lo = cnt_lo + jnp.where(sub == 0, jnp.sum(picked[-1], axis=0, keepdims=True), zero)
    cnt_hi = picked[1]
    total = jnp.sum(cnt_lo + cnt_hi, axis=0, keepdims=True)
    tie = jnp.where(total != float(k), one, tie)

    cnt_keys = jnp.zeros((tiles, SUBLANES, LANES), F32)
    for a in range(k):
        src = cnt_lo if a < SUBLANES else cnt_hi
        row = a % SUBLANES
        cnt_a = jnp.broadcast_to(src[row:row + 1, :], (SUBLANES, LANES))
        cnt_keys = jnp.where(s1r == v0[a], cnt_a, cnt_keys)
    rank1 = jnp.full((tiles, SUBLANES, LANES), float(k), F32)
    for a in range(k - 1, -1, -1):
        rank1 = jnp.where(v1[a] <= s2r, float(a), rank1)
    e2 = jnp.exp(s2r - v1[0])
    c1 = jnp.exp(s1r - v0[0]) / z
    flat = lambda t: t.reshape(nk, LANES)
    return flat(rank1), flat(e2), flat(cnt_keys), flat(c1), tie


def _fold_keys_kernel(keys_ref, wq_ref, o_ref):
    o_ref[...] = lax.dot_general(keys_ref[...], wq_ref[...], (((1,), (1,)), ((), ())),
                                 preferred_element_type=F32, precision=HIGHEST).astype(BF16)


def _fold_keys(keys, wq):
    d, qdim = wq.shape
    nk, half = keys.shape[1:]
    return pl.pallas_call(
        _fold_keys_kernel,
        grid=(qdim // half,),
        in_specs=[pl.BlockSpec((None, nk, half), lambda j: (j % 2, 0, 0)),
                  pl.BlockSpec((d, half), lambda j: (0, j))],
        out_specs=pl.BlockSpec((nk, d), lambda j: (j, 0)),
        out_shape=jax.ShapeDtypeStruct((qdim // half * nk, d), BF16),
        compiler_params=_cparams(("arbitrary",)),
        name="fold_keys",
    )(keys, wq)


def _route_kernel(h_ref, nw_ref, sc_ref, sh_ref, ws_ref,
                  r2_ref, e2_ref, cnt_ref, c1_ref, score_ref):
    u2 = (_rmsnorm_rows(h_ref[...], nw_ref[...]) * (1.0 + sc_ref[...]) + sh_ref[...]).astype(BF16)
    score_ref[...] = lax.dot_general(ws_ref[...], u2, (((1,), (1,)), ((), ())),
                                     preferred_element_type=F32)
    n_col = h_ref.shape[0] // LANES
    nk = PEER_N_KEYS

    def body(it, carry):
        head = it // n_col
        tcol = it % n_col
        lanes = pl.ds(pl.multiple_of(tcol * LANES, LANES), LANES)
        rows0 = pl.ds(pl.multiple_of(head * (2 * nk), nk), nk)
        rows1 = pl.ds(pl.multiple_of(head * (2 * nk) + nk, nk), nk)
        s1 = score_ref[rows0, lanes]
        s2 = score_ref[rows1, lanes]

        def store(rank1, e2, cnt_keys, c1):
            r2_ref[head, tcol] = pltpu.bitcast(rank1.astype(BF16), jnp.int32)
            e2_ref[head, tcol] = pltpu.bitcast(e2.astype(BF16), jnp.int32)
            cnt_ref[head, tcol] = cnt_keys
            c1_ref[head, tcol] = c1

        rank1, e2, cnt_keys, c1, tie = _route_fast(s1, s2)
        store(rank1, e2, cnt_keys, c1)

        @pl.when(jnp.max(tie) > 0.0)
        def _():
            store(*_route_exact(score_ref[rows0, lanes], score_ref[rows1, lanes]))

        return carry

    lax.fori_loop(0, PEER_HEADS * n_col, body, 0)


def _route(h, norm_w, sc, sh, w_score):
    bsz, s, d = h.shape
    tt = PEER_TOK
    sel = lambda rows: pl.BlockSpec((None, PEER_HEADS, tt // LANES, rows, LANES), lambda b, t: (b, 0, t, 0, 0))
    out = lambda rows, dt: jax.ShapeDtypeStruct((bsz, PEER_HEADS, s // LANES, rows, LANES), dt)
    packed = PEER_N_KEYS * 2 // 4
    n_scores = w_score.shape[0]
    return pl.pallas_call(
        _route_kernel,
        grid=(bsz, s // tt),
        in_specs=[pl.BlockSpec((None, tt, d), lambda b, t: (b, t, 0)),
                  pl.BlockSpec((1, d), lambda b, t: (0, 0)),
                  pl.BlockSpec((None, 1, d), lambda b, t: (b, 0, 0)),
                  pl.BlockSpec((None, 1, d), lambda b, t: (b, 0, 0)),
                  pl.BlockSpec((n_scores, d), lambda b, t: (0, 0))],
        out_specs=[sel(packed), sel(packed), sel(PEER_N_KEYS), sel(PEER_N_KEYS)],
        out_shape=[out(packed, jnp.int32), out(packed, jnp.int32),
                   out(PEER_N_KEYS, F32), out(PEER_N_KEYS, F32)],
        scratch_shapes=[pltpu.VMEM((n_scores, tt), F32)],
        compiler_params=_cparams(("parallel", "parallel")),
        name="route",
    )(h, norm_w.reshape(1, d), sc, sh, w_score)


def _peer_kernel(h_ref, nw_ref, sc_ref, sh_ref, gate_ref, nf_ref,
                 u_first_ref, u_next_ref, vt_prev_ref, vt_last_ref,
                 r2_ref, e2_ref, cnt_ref, c1_ref, o_ref,
                 x_ref, at_even_ref, at_odd_ref, wg_even_ref, wg_odd_ref, acc_ref,
                 *, final_norm, n_blocks):
    c = pl.program_id(2)
    tt, d = h_ref.shape
    nk = PEER_N_KEYS
    ncol = tt // LANES
    gk = PEER_GROUP_KEYS
    dq = d // PEER_TRIPS
    eq = PEER_IBLK * nk // PEER_TRIPS

    def pre_activations(u_packed, at_ref, rows):
        pre = jnp.dot(pltpu.bitcast(u_packed, BF16), x_ref[...], preferred_element_type=F32)
        for tcol in range(ncol):
            at_ref[tcol, rows, :] = pre[:, tcol * LANES:(tcol + 1) * LANES]

    @pl.when(c == 0)
    def _():
        u2 = _rmsnorm_rows(h_ref[...], nw_ref[...]) * (1.0 + sc_ref[...]) + sh_ref[...]
        x_ref[...] = u2.T.astype(BF16)
        acc_ref[...] = jnp.zeros_like(acc_ref)
        wg_odd_ref[...] = jnp.zeros(wg_odd_ref.shape, BF16)
        pre_activations(u_first_ref[...], at_even_ref, slice(None))

    def row_tile(ref, hd, tcol, key):
        return jnp.broadcast_to(ref[hd, tcol, key:key + 1, :], (nk, LANES)).astype(BF16)

    def gates(wg_ref):
        return jnp.concatenate([wg_ref[tcol] for tcol in range(ncol)], axis=1)

    def step(at_ref, at_next_ref, wg_ref, wg_prev_ref):
        def output_slice(q):
            out_rows = pl.ds(pl.multiple_of(q * dq, dq), dq)
            vt_rows = pl.ds(pl.multiple_of(q * (dq // 2), dq // 2), dq // 2)
            acc_ref[out_rows, :] += jnp.dot(pltpu.bitcast(vt_prev_ref[vt_rows, :], BF16), gates(wg_prev_ref),
                                            preferred_element_type=F32)

        def next_slice(q):
            u_rows = pl.ds(pl.multiple_of(q * (eq // 2), eq // 2), eq // 2)
            pre_activations(u_next_ref[u_rows, :], at_next_ref, pl.ds(pl.multiple_of(q * eq, eq), eq))

        cols = ncol // PEER_TRIPS

        def body(q, carry):
            for j in range(cols):
                tcol = q * cols + j
                for k0 in range(0, PEER_IBLK, gk):
                    w = [jnp.zeros((nk, LANES), BF16) for _ in range(gk)]
                    for hd in range(PEER_HEADS):
                        r2 = pltpu.bitcast(r2_ref[hd, tcol], BF16)
                        e2 = pltpu.bitcast(e2_ref[hd, tcol], BF16)
                        for k in range(gk):
                            sel = r2 < row_tile(cnt_ref, hd, tcol, k0 + k)
                            w[k] = w[k] + (jnp.where(sel, e2, jnp.zeros_like(e2))
                                           * row_tile(c1_ref, hd, tcol, k0 + k))
                    for k in range(gk):
                        rows = slice((k0 + k) * nk, (k0 + k + 1) * nk)
                        a = at_ref[tcol, rows, :]
                        act = 0.5 * a * (1.0 + lax.erf(a * np.float32(math.sqrt(0.5))))
                        wg_ref[tcol, rows, :] = w[k] * act.astype(BF16)
            output_slice(q)
            next_slice(q)
            return carry

        for q in range(PEER_TRIPS):
            body(q, 0)

    @pl.when(c % 2 == 0)
    def _():
        step(at_even_ref, at_odd_ref, wg_even_ref, wg_odd_ref)

    @pl.when(c % 2 == 1)
    def _():
        step(at_odd_ref, at_even_ref, wg_odd_ref, wg_even_ref)

    @pl.when(c == n_blocks - 1)
    def _():
        wg_last_ref = wg_odd_ref if (n_blocks - 1) % 2 else wg_even_ref
        total = acc_ref[...] + jnp.dot(pltpu.bitcast(vt_last_ref[...], BF16), gates(wg_last_ref),
                                       preferred_element_type=F32)
        hh = h_ref[...] + gate_ref[...] * total.T
        o_ref[...] = _rmsnorm_rows(hh, nf_ref[...]) if final_norm else hh


def _peer(h, norm_w, sc, sh, gate2, normf_w, u_bf, vt_bf, r2, e2, cnt, c1, final_norm):
    bsz, s, d = h.shape
    tt = PEER_TOK
    ec = PEER_IBLK * PEER_N_KEYS
    n_exp = 2 * u_bf.shape[0]
    vec = pl.BlockSpec((1, d), lambda b, t, c: (0, 0))
    bvec = pl.BlockSpec((None, 1, d), lambda b, t, c: (b, 0, 0))
    ncol = tt // LANES
    selj = pl.BlockSpec((None, PEER_HEADS, ncol, PEER_N_KEYS // 2, LANES), lambda b, t, c: (b, 0, t, 0, 0))
    seli = pl.BlockSpec((None, PEER_HEADS, ncol, PEER_IBLK, LANES), lambda b, t, c: (b, 0, t, c, 0))
    n_blocks = n_exp // ec
    return pl.pallas_call(
        functools.partial(_peer_kernel, final_norm=final_norm, n_blocks=n_blocks),
        grid=(bsz, s // tt, n_blocks),
        in_specs=[pl.BlockSpec((None, tt, d), lambda b, t, c: (b, t, 0)),
                  vec, bvec, bvec, bvec, vec,
                  pl.BlockSpec((ec // 2, d), lambda b, t, c: (0, 0)),
                  pl.BlockSpec((ec // 2, d), lambda b, t, c: (jnp.minimum(c + 1, n_blocks - 1), 0)),
                  pl.BlockSpec((d // 2, ec), lambda b, t, c: (0, jnp.maximum(c - 1, 0))),
                  pl.BlockSpec((d // 2, ec), lambda b, t, c: (0, n_blocks - 1)),
                  selj, selj, seli, seli],
        out_specs=pl.BlockSpec((None, tt, d), lambda b, t, c: (b, t, 0)),
        out_shape=jax.ShapeDtypeStruct((bsz, s, d), F32),
        scratch_shapes=[pltpu.VMEM((d, tt), BF16),
                        pltpu.VMEM((ncol, ec, LANES), F32),
                        pltpu.VMEM((ncol, ec, LANES), F32),
                        pltpu.VMEM((ncol, ec, LANES), BF16),
                        pltpu.VMEM((ncol, ec, LANES), BF16),
                        pltpu.VMEM((d, tt), F32)],
        compiler_params=_cparams(("parallel", "parallel", "arbitrary")),
        name="peer",
    )(h, norm_w.reshape(1, d), sc, sh, gate2, normf_w.reshape(1, d), u_bf, u_bf, vt_bf, vt_bf,
      r2, e2, cnt, c1)


def _rope_tables(s):
    half = ATT_HEAD_DIM // 2
    inv_freq = ROPE_THETA ** (-jnp.arange(half, dtype=F32) * 2.0 / ATT_HEAD_DIM)
    ang = jnp.arange(s, dtype=F32)[:, None] * inv_freq[None, :]
    cos, sin = jnp.cos(ang), jnp.sin(ang)
    reps = LANES // ATT_HEAD_DIM
    cos_t = jnp.tile(jnp.concatenate([cos, cos], axis=-1), (1, reps))
    sin_t = jnp.tile(jnp.concatenate([-sin, sin], axis=-1), (1, reps))
    return cos_t, sin_t


def _pack_kernel(x_ref, o_ref, *, transpose):
    x = x_ref[...]
    if transpose:
        x = x.T
    o_ref[...] = pltpu.bitcast(x.astype(BF16), jnp.int32)


def _pack_table(x, transpose):
    n, d = x.shape
    blk = PEER_IBLK * PEER_N_KEYS
    if transpose:
        out_spec = pl.BlockSpec((d // 2, blk), lambda j: (0, j))
        out_shape = jax.ShapeDtypeStruct((d // 2, n), jnp.int32)
    else:
        out_spec = pl.BlockSpec((blk // 2, d), lambda j: (j, 0))
        out_shape = jax.ShapeDtypeStruct((n // 2, d), jnp.int32)
    return pl.pallas_call(
        functools.partial(_pack_kernel, transpose=transpose),
        grid=(n // blk,),
        in_specs=[pl.BlockSpec((blk, d), lambda j: (j, 0))],
        out_specs=out_spec,
        out_shape=out_shape,
        compiler_params=_cparams(("parallel",)),
        name="pack_vt" if transpose else "pack_u",
    )(x)


def _permute_in_proj(w_in):
    n_if = 2 * MLSTM_HEADS
    lo = ATT_COLS + MQK_COLS + 2 * MLSTM_W
    pad = jnp.zeros((w_in.shape[0], IF_COLS - n_if), w_in.dtype)
    return jnp.concatenate([w_in[:, :lo], w_in[:, lo + n_if:], w_in[:, lo:lo + n_if], pad], axis=1)


def kernel(x, c, w_ada, b_ada, norm1_w, w_in, conv_w, att_sinks, i_bias, f_bias, mlstm_norm_w,
           w_att_branch, w_mlstm_branch, w_out, norm2_w, peer_w_query, peer_sub_keys, peer_u,
           peer_v, norm_f_w):
    depth = w_ada.shape[0]
    bsz, s, d = x.shape
    cos_t, sin_t = _rope_tables(s)
    h = x
    for l in range(depth):
        mod = _ada(c, w_ada[l], b_ada[l]).reshape(bsz, 6, 1, d)
        sh1, sc1, g1, sh2, sc2, g2 = (mod[:, i] for i in range(6))

        w_perm = _permute_in_proj(w_in[l]).astype(BF16)
        att_in, mqk, mv, mo, gates, gif = _inproj(h, norm1_w[l], sc1, sh1, w_perm)
        att = _attention(att_in, att_sinks[l], cos_t, sin_t)
        bias_row = jnp.concatenate(
            [i_bias[l], f_bias[l], jnp.zeros((IF_COLS - 2 * MLSTM_HEADS,), F32)]).reshape(1, IF_COLS)
        mls = _mlstm(mqk, mv, mo, gif, conv_w[l], bias_row, mlstm_norm_w[l])
        h = _merge(h, att, mls, gates, g1, w_att_branch[l].astype(BF16),
                   w_mlstm_branch[l].astype(BF16), w_out[l].astype(BF16))

        w_score = _fold_keys(peer_sub_keys[l], peer_w_query[l])
        r2, e2, cnt, c1 = _route(h, norm2_w[l], sc2, sh2, w_score)
        h = _peer(h, norm2_w[l], sc2, sh2, g2, norm_f_w, _pack_table(peer_u[l], transpose=False),
                  _pack_table(peer_v[l], transpose=True), r2, e2, cnt, c1,
                  final_norm=(l == depth - 1))
    return h
```

```python
import functools
import math

import numpy as np
import jax
import jax.numpy as jnp
from jax import lax
from jax.experimental import pallas as pl
from jax.experimental.pallas import tpu as pltpu

F32 = jnp.float32
BF16 = jnp.bfloat16
HIGHEST = lax.Precision.HIGHEST

D_MODEL = 1024
ATT_HEADS = 8
ATT_KV_HEADS = 2
ATT_HEAD_DIM = 64
ATT_BLOCK = 128
ROPE_THETA = 10000.0
MLSTM_HEADS = 4
MLSTM_HEAD_DIM = 128
MLSTM_CHUNK = 128
CONV_WIDTH = 4
ATT_Q_W = ATT_HEADS * ATT_HEAD_DIM
ATT_KV_W = ATT_KV_HEADS * ATT_HEAD_DIM
MLSTM_W = MLSTM_HEADS * MLSTM_HEAD_DIM
PEER_HEADS = 8
PEER_N_KEYS = 128
PEER_TOPK = 16
EPS = 1e-6

LANES = 128
SUBLANES = 8
VMEM_LIMIT = 48 * 1024 * 1024

ATT_COLS = ATT_Q_W + 2 * ATT_KV_W
MQK_COLS = 2 * MLSTM_W
GATE_COLS = 2 * D_MODEL
IF_COLS = LANES
OFF_ATT = 0
OFF_MQK = OFF_ATT + ATT_COLS
OFF_MV = OFF_MQK + MQK_COLS
OFF_MO = OFF_MV + MLSTM_W
OFF_G = OFF_MO + MLSTM_W
OFF_IF = OFF_G + GATE_COLS
IN_COLS = OFF_IF + IF_COLS

TOK_TILE = 512
PEER_TOK = 512
PEER_IBLK = 8
PEER_TRIPS = 2
PEER_GROUP_KEYS = 1
NEG_INF = float("-inf")


def _cparams(sem):
    return pltpu.CompilerParams(dimension_semantics=sem, vmem_limit_bytes=VMEM_LIMIT)


def _rmsnorm_rows(x, w):
    return x * lax.rsqrt(jnp.mean(x * x, axis=-1, keepdims=True) + EPS) * w


def _ada_kernel(c_ref, w_ref, b_ref, o_ref):
    c = c_ref[...]
    act = c * jax.nn.sigmoid(c)
    o_ref[...] = jnp.dot(act, w_ref[...], preferred_element_type=F32, precision=HIGHEST) + b_ref[...]


def _ada(c, w, b):
    bsz, d = c.shape
    n_out = w.shape[1]
    return pl.pallas_call(
        _ada_kernel,
        grid=(n_out // d,),
        in_specs=[pl.BlockSpec((bsz, d), lambda j: (0, 0)),
                  pl.BlockSpec((d, d), lambda j: (0, j)),
                  pl.BlockSpec((1, d), lambda j: (0, j))],
        out_specs=pl.BlockSpec((bsz, d), lambda j: (0, j)),
        out_shape=jax.ShapeDtypeStruct((bsz, n_out), F32),
        compiler_params=_cparams(("arbitrary",)),
        name="ada",
    )(c, w, b.reshape(1, n_out))


def _inproj_kernel(x_ref, nw_ref, sc_ref, sh_ref, w_ref,
                   att_ref, mqk_ref, mv_ref, mo_ref, g_ref, if_ref):
    y = _rmsnorm_rows(x_ref[...], nw_ref[...])
    u = (y * (1.0 + sc_ref[...]) + sh_ref[...]).astype(BF16)

    def proj(lo, width):
        return jnp.dot(u, w_ref[:, lo:lo + width], preferred_element_type=F32)

    att_ref[...] = proj(OFF_ATT, ATT_COLS).astype(BF16)
    mqk_ref[...] = proj(OFF_MQK, MQK_COLS).astype(BF16)
    mv_ref[...] = proj(OFF_MV, MLSTM_W).astype(BF16)
    mo_ref[...] = proj(OFF_MO, MLSTM_W).astype(BF16)
    g_ref[...] = proj(OFF_G, GATE_COLS).astype(BF16)
    if_ref[...] = proj(OFF_IF, IF_COLS)


def _inproj(x, norm_w, sc, sh, w_perm):
    bsz, s, d = x.shape
    tm = TOK_TILE
    tok = lambda width: pl.BlockSpec((None, tm, width), lambda b, i: (b, i, 0))
    vec = pl.BlockSpec((None, 1, d), lambda b, i: (b, 0, 0))
    widths = (ATT_COLS, MQK_COLS, MLSTM_W, MLSTM_W, GATE_COLS, IF_COLS)
    dtypes = (BF16, BF16, BF16, BF16, BF16, F32)
    return pl.pallas_call(
        _inproj_kernel,
        grid=(bsz, s // tm),
        in_specs=[tok(d),
                  pl.BlockSpec((1, d), lambda b, i: (0, 0)),
                  vec, vec,
                  pl.BlockSpec((d, IN_COLS), lambda b, i: (0, 0))],
        out_specs=[tok(w) for w in widths],
        out_shape=[jax.ShapeDtypeStruct((bsz, s, w), dt) for w, dt in zip(widths, dtypes)],
        compiler_params=_cparams(("parallel", "parallel")),
        name="inproj",
    )(x, norm_w.reshape(1, d), sc, sh, w_perm)


def _attn_kernel(sink_ref, cur_ref, prev_ref, cosc_ref, sinc_ref, cosp_ref, sinp_ref, o_ref):
    for b in range(cur_ref.shape[0]):
        _attn_block(sink_ref, cur_ref.at[b], prev_ref.at[b], cosc_ref, sinc_ref, cosp_ref, sinp_ref,
                    o_ref.at[b])


def _attn_block(sink_ref, cur_ref, prev_ref, cosc_ref, sinc_ref, cosp_ref, sinp_ref, o_ref):
    blk = ATT_BLOCK
    n = pl.program_id(0)
    lane = lax.broadcasted_iota(jnp.int32, (blk, LANES), 1)
    first_half = (lane & (ATT_HEAD_DIM - 1)) < (ATT_HEAD_DIM // 2)
    low = lane < ATT_HEAD_DIM

    def rope(xf, cos, sin):
        rot = jnp.where(first_half, pltpu.roll(xf, LANES - ATT_HEAD_DIM // 2, 1),
                        pltpu.roll(xf, ATT_HEAD_DIM // 2, 1))
        return xf * cos + rot * sin

    def dup(xf, g):
        sw = pltpu.roll(xf, ATT_HEAD_DIM, 1)
        return jnp.where(low, xf, sw) if g == 0 else jnp.where(low, sw, xf)

    cosc, sinc = cosc_ref[...], sinc_ref[...]
    cosp, sinp = cosp_ref[...], sinp_ref[...]
    k_cur = rope(cur_ref[:, ATT_Q_W:ATT_Q_W + ATT_KV_W].astype(F32), cosc, sinc)
    k_prev = rope(prev_ref[:, ATT_Q_W:ATT_Q_W + ATT_KV_W].astype(F32), cosp, sinp)
    v_cur = cur_ref[:, ATT_Q_W + ATT_KV_W:ATT_COLS].astype(F32)
    v_prev = prev_ref[:, ATT_Q_W + ATT_KV_W:ATT_COLS].astype(F32)

    qi = lax.broadcasted_iota(jnp.int32, (blk, 2 * blk), 0)
    ki = lax.broadcasted_iota(jnp.int32, (blk, 2 * blk), 1)
    q_lim = jnp.where(n > 0, qi, 2 * blk)
    mask = ((ki < blk) & (ki > q_lim)) | ((ki >= blk) & ((ki - blk) <= qi))

    scale = ATT_HEAD_DIM ** -0.5
    group = ATT_HEADS // ATT_KV_HEADS
    for g in range(ATT_KV_HEADS):
        kk = jnp.concatenate([dup(k_prev, g), dup(k_cur, g)], axis=0).astype(BF16)
        vv = jnp.concatenate([dup(v_prev, g), dup(v_cur, g)], axis=0).astype(BF16)
        for pair in range(group // 2):
            j = g * (group // 2) + pair
            qp = rope(cur_ref[:, j * LANES:(j + 1) * LANES].astype(F32), cosc, sinc) * scale
            outs = []
            for half in range(2):
                head = 2 * j + half
                qh = jnp.where(low, qp, 0.0) if half == 0 else jnp.where(low, 0.0, qp)
                s = lax.dot_general(qh.astype(BF16), kk, (((1,), (1,)), ((), ())),
                                    preferred_element_type=F32)
                s = jnp.where(mask, s, NEG_INF)
                sink = sink_ref[head]
                m = jnp.maximum(jnp.max(s, axis=-1, keepdims=True), sink)
                p = jnp.exp(s - m)
                denom = jnp.sum(p, axis=-1, keepdims=True) + jnp.exp(sink - m)
                o = jnp.dot(p.astype(BF16), vv, preferred_element_type=F32)
                outs.append(o / denom)
            o_ref[:, j * LANES:(j + 1) * LANES] = jnp.where(low, outs[0], outs[1]).astype(BF16)


def _attention(att_in, sinks, cos_t, sin_t):
    bsz, s, _ = att_in.shape
    blk = ATT_BLOCK
    nb = s // blk
    prev = lambda n: jnp.maximum(n - 1, 0)
    return pl.pallas_call(
        _attn_kernel,
        grid=(nb,),
        in_specs=[pl.BlockSpec(memory_space=pltpu.SMEM),
                  pl.BlockSpec((bsz, blk, ATT_COLS), lambda n: (0, n, 0)),
                  pl.BlockSpec((bsz, blk, ATT_COLS), lambda n: (0, prev(n), 0)),
                  pl.BlockSpec((blk, LANES), lambda n: (n, 0)),
                  pl.BlockSpec((blk, LANES), lambda n: (n, 0)),
                  pl.BlockSpec((blk, LANES), lambda n: (prev(n), 0)),
                  pl.BlockSpec((blk, LANES), lambda n: (prev(n), 0))],
        out_specs=pl.BlockSpec((bsz, blk, ATT_Q_W), lambda n: (0, n, 0)),
        out_shape=jax.ShapeDtypeStruct((bsz, s, ATT_Q_W), BF16),
        compiler_params=_cparams(("parallel",)),
        name="attn",
    )(sinks, att_in, att_in, cos_t, sin_t, cos_t, sin_t)


def _mlstm_kernel(mqk_ref, mv_ref, mo_ref, if_ref, convw_ref, bias_ref, nw_ref,
                  o_ref, buf_ref, c_ref, n_ref, m_ref):
    L = MLSTM_CHUNK
    d = MLSTM_HEAD_DIM
    tail = SUBLANES

    @pl.when(pl.program_id(1) == 0)
    def _():
        buf_ref[0:tail, :] = jnp.zeros((tail, 2 * MLSTM_W), F32)
        c_ref[...] = jnp.zeros_like(c_ref)
        n_ref[...] = jnp.zeros_like(n_ref)
        m_ref[...] = jnp.zeros_like(m_ref)

    buf_ref[tail:tail + L, :] = mqk_ref[...].astype(F32)
    y = jnp.zeros((L, 2 * MLSTM_W), F32)
    for j in range(CONV_WIDTH):
        off = tail - (CONV_WIDTH - 1) + j
        y = y + convw_ref[j:j + 1, :] * buf_ref[off:off + L, :]
    buf_ref[0:tail, :] = buf_ref[L:L + tail, :]
    y = y * jax.nn.sigmoid(y)
    q_all = y[:, :MLSTM_W]
    k_all = y[:, MLSTM_W:] * (d ** -0.5)

    gates = if_ref[...] + bias_ref[...]
    lf = jnp.minimum(gates, 0.0) - jnp.log1p(jnp.exp(-jnp.abs(gates)))
    row = lax.broadcasted_iota(jnp.int32, (L, L), 0)
    col = lax.broadcasted_iota(jnp.int32, (L, L), 1)
    causal = col <= row
    tril = causal.astype(F32)
    triu = (row <= col).astype(F32)
    bcum_c = jnp.dot(tril, lf, preferred_element_type=F32, precision=HIGHEST)
    gates_t = gates.T
    bcum_r = jnp.dot(lf.T, triu, preferred_element_type=F32, precision=HIGHEST)

    for h in range(MLSTM_HEADS):
        hs = slice(h * d, (h + 1) * d)
        fl = MLSTM_HEADS + h
        q = q_all[:, hs]
        k = k_all[:, hs]
        v = mv_ref[:, hs]
        ic_c = gates[:, h:h + 1]
        ic_r = gates_t[h:h + 1, :]
        b_c = bcum_c[:, fl:fl + 1]
        b_r = bcum_r[fl:fl + 1, :]
        g_tot = b_c[L - 1:L, :]
        c_prev = c_ref[h]
        n_prev = n_ref[h:h + 1, :]
        m_prev = m_ref[h:h + 1, 0:1]

        dmat = jnp.where(causal, b_c - b_r + ic_r, NEG_INF)
        m_inter = b_c + m_prev
        m_out = jnp.maximum(m_inter, jnp.max(dmat, axis=-1, keepdims=True))
        qb = q.astype(BF16)
        kb = k.astype(BF16)
        qk = lax.dot_general(qb, kb, (((1,), (1,)), ((), ())), preferred_element_type=F32)
        w_intra = jnp.exp(dmat - m_out) * qk
        w_inter = jnp.exp(m_inter - m_out)
        num = (jnp.dot(w_intra.astype(BF16), v, preferred_element_type=F32)
               + w_inter * jnp.dot(qb, c_prev.astype(BF16), preferred_element_type=F32))
        den = (jnp.sum(w_intra, axis=-1, keepdims=True)
               + w_inter * jnp.sum(q * n_prev, axis=-1, keepdims=True))
        hid = num / jnp.maximum(jnp.abs(den), jnp.exp(-m_out))
        hid = jax.nn.sigmoid(mo_ref[:, hs].astype(F32)) * hid
        hid = hid * lax.rsqrt(jnp.mean(hid * hid, axis=-1, keepdims=True) + EPS)
        o_ref[:, hs] = (hid * nw_ref[:, hs]).astype(BF16)

        a_c = g_tot - b_c + ic_c
        m_new = jnp.maximum(g_tot + m_prev, jnp.max(a_c, axis=0, keepdims=True))
        decay = jnp.exp(g_tot + m_prev - m_new)
        wk = jnp.exp(a_c - m_new) * k
        c_ref[h] = decay * c_prev + lax.dot_general(
            wk.astype(BF16), v, (((0,), (0,)), ((), ())), preferred_element_type=F32)
        n_ref[h:h + 1, :] = decay * n_prev + jnp.sum(wk, axis=0, keepdims=True)
        m_ref[h:h + 1, :] = jnp.broadcast_to(m_new, (1, LANES))


def _mlstm(mqk, mv, mo, gif, conv_w, bias_row, norm_w):
    bsz, s, _ = mqk.shape
    L = MLSTM_CHUNK
    tok = lambda width: pl.BlockSpec((None, L, width), lambda b, c: (b, c, 0))
    full = lambda shape: pl.BlockSpec(shape, lambda b, c: (0, 0))
    return pl.pallas_call(
        _mlstm_kernel,
        grid=(bsz, s // L),
        in_specs=[tok(MQK_COLS), tok(MLSTM_W), tok(MLSTM_W), tok(IF_COLS),
                  full((CONV_WIDTH, MQK_COLS)), full((1, IF_COLS)), full((1, MLSTM_W))],
        out_specs=tok(MLSTM_W),
        out_shape=jax.ShapeDtypeStruct((bsz, s, MLSTM_W), BF16),
        scratch_shapes=[pltpu.VMEM((L + SUBLANES, MQK_COLS), F32),
                        pltpu.VMEM((MLSTM_HEADS, MLSTM_HEAD_DIM, MLSTM_HEAD_DIM), F32),
                        pltpu.VMEM((SUBLANES, MLSTM_HEAD_DIM), F32),
                        pltpu.VMEM((SUBLANES, LANES), F32)],
        compiler_params=_cparams(("arbitrary", "arbitrary")),
        name="mlstm",
    )(mqk, mv, mo, gif, conv_w, bias_row, norm_w.reshape(1, MLSTM_W))


def _merge_kernel(x_ref, att_ref, mls_ref, g_ref, gate_ref, wa_ref, wm_ref, wo_ref, h_ref):
    a = jnp.dot(att_ref[...], wa_ref[...], preferred_element_type=F32)
    m = jnp.dot(mls_ref[...], wm_ref[...], preferred_element_type=F32)
    ga = g_ref[:, :D_MODEL].astype(F32)
    gm = g_ref[:, D_MODEL:].astype(F32)
    merged = jax.nn.sigmoid(ga) * a + jax.nn.sigmoid(gm) * m
    y = jnp.dot(merged.astype(BF16), wo_ref[...], preferred_element_type=F32)
    h_ref[...] = x_ref[...] + gate_ref[...] * y


def _merge(x, att, mls, g, gate1, w_att, w_mls, w_out):
    bsz, s, d = x.shape
    tm = TOK_TILE
    tok = lambda width: pl.BlockSpec((None, tm, width), lambda b, i: (b, i, 0))
    full = lambda shape: pl.BlockSpec(shape, lambda b, i: (0, 0))
    return pl.pallas_call(
        _merge_kernel,
        grid=(bsz, s // tm),
        in_specs=[tok(d), tok(ATT_Q_W), tok(MLSTM_W), tok(GATE_COLS),
                  pl.BlockSpec((None, 1, d), lambda b, i: (b, 0, 0)),
                  full((ATT_Q_W, d)), full((MLSTM_W, d)), full((d, d))],
        out_specs=tok(d),
        out_shape=jax.ShapeDtypeStruct((bsz, s, d), F32),
        compiler_params=_cparams(("parallel", "parallel")),
        name="merge",
    )(x, att, mls, g, gate1, w_att, w_mls, w_out)


def _top16(s, iota_f):
    n_keys = s.shape[0]
    work = s
    rank = jnp.full(s.shape, float(PEER_TOPK), F32)
    vals = []
    for a in range(PEER_TOPK):
        m = jnp.max(work, axis=0, keepdims=True)
        idx = jnp.min(jnp.where(work == m, iota_f, float(n_keys)), axis=0, keepdims=True)
        sel = iota_f == idx
        rank = jnp.where(sel, float(a), rank)
        work = jnp.where(sel, NEG_INF, work)
        vals.append(m)
    return vals, rank


def _route_exact(s1, s2):
    k = PEER_TOPK
    nk = PEER_N_KEYS
    iota_keys = lax.broadcasted_iota(jnp.int32, (nk, LANES), 0).astype(F32)
    iota_k = lax.broadcasted_iota(jnp.int32, (k, LANES), 0).astype(F32)
    vals0, rank0 = _top16(s1, iota_keys)
    vals1, rank1 = _top16(s2, iota_keys)
    v0 = jnp.concatenate(vals0, axis=0)

    cnt = jnp.zeros((k, LANES), F32)
    front = v0 + vals1[0]
    c_max = vals0[0] + vals1[0]
    z = jnp.zeros((1, LANES), F32)
    for _ in range(k):
        m = jnp.max(front, axis=0, keepdims=True)
        idx = jnp.min(jnp.where(front == m, iota_k, float(k)), axis=0, keepdims=True)
        sel = iota_k == idx
        cnt = cnt + jnp.where(sel, 1.0, 0.0)
        z = z + jnp.exp(m - c_max)
        taken = jnp.sum(jnp.where(sel, cnt, 0.0), axis=0, keepdims=True)
        nxt = jnp.full((1, LANES), NEG_INF, F32)
        for b in range(1, k):
            nxt = jnp.where(taken == float(b), vals1[b], nxt)
        front = jnp.where(sel, v0 + nxt, front)

    cnt_keys = jnp.zeros((nk, LANES), F32)
    for a in range(k):
        cnt_keys = jnp.where(rank0 == float(a), cnt[a:a + 1, :], cnt_keys)
    return rank1, jnp.exp(s2 - vals1[0]), cnt_keys, jnp.exp(s1 - vals0[0]) / z


def _sort_network(n):
    def merge(lo, hi, r):
        step = r * 2
        if step < hi - lo:
            yield from merge(lo, hi, step)
            yield from merge(lo + r, hi, step)
            yield from [(i, i + r) for i in range(lo + r, hi - r, step)]
        else:
            yield (lo, lo + r)

    def sort(lo, hi):
        if hi - lo >= 1:
            mid = lo + (hi - lo) // 2
            yield from sort(lo, mid)
            yield from sort(mid + 1, hi)
            yield from merge(lo, hi, 1)

    return list(sort(0, n - 1))


def _top16_values(x):
    n = PEER_TOPK
    x = list(x) + [None] * (n - len(x))

    def top(a, b):
        return b if a is None else a if b is None else jnp.maximum(a, b)

    def cmpx(i, j):
        a, b = x[i], x[j]
        if b is None:
            return
        if a is None:
            x[i], x[j] = b, None
        else:
            x[i], x[j] = jnp.maximum(a, b), jnp.minimum(a, b)

    for i, j in _sort_network(n):
        cmpx(i, j)
    shift = SUBLANES // 2
    while shift >= 1:
        y = [None if t is None else pltpu.roll(t, shift, 0) for t in x]
        x = [top(x[i], y[n - 1 - i]) for i in range(n)]
        gap = n // 2
        while gap >= 1:
            for i in range(n):
                if i & gap == 0:
                    cmpx(i, i + gap)
            gap //= 2
        shift //= 2
    return x


def _route_fast(s1, s2):
    k = PEER_TOPK
    nk = PEER_N_KEYS
    tiles = nk // SUBLANES
    s1r = s1.reshape(tiles, SUBLANES, LANES)
    s2r = s2.reshape(tiles, SUBLANES, LANES)
    v0 = _top16_values([s1r[t] for t in range(tiles)])
    v1 = _top16_values([s2r[t] for t in range(tiles)])
    sub = lax.broadcasted_iota(jnp.int32, (SUBLANES, LANES), 0)
    one = jnp.ones((SUBLANES, LANES), F32)
    zero = jnp.zeros((SUBLANES, LANES), F32)

    tie = zero
    for v, sr in ((v0, s1r), (v1, s2r)):
        for a in range(k - 1):
            tie = jnp.where(v[a] == v[a + 1], one, tie)
        above = jnp.sum(jnp.where(sr >= v[k - 1], 1.0, 0.0), axis=0)
        tie = jnp.where(jnp.sum(above, axis=0, keepdims=True) != float(k), one, tie)

    def by_sublane(vals):
        out = vals[SUBLANES - 1]
        for a in range(SUBLANES - 2, -1, -1):
            out = jnp.where(sub == a, vals[a], out)
        return out

    v0_lo, v0_hi, v1_hi = by_sublane(v0[:SUBLANES]), by_sublane(v0[SUBLANES:]), by_sublane(v1[SUBLANES:])
    cand = [v0_lo + v1[0], v0_hi + v1[0]]
    for b in range(1, SUBLANES):
        cand.append(jnp.where(sub < (k // (b + 1)), v0_lo + v1[b], NEG_INF))
    cand.append(v0[0] + v1_hi)
    best = _top16_values(cand)
    z = one
    for a in range(1, k):
        z = z + jnp.exp(best[a] - best[0])
    picked = [jnp.where(t >= best[k - 1], one, zero) for t in cand]
    cnt_lo = picked[0]
    for t in picked[2:-1]:
        cnt_lo = cnt_lo + t
    cnt_lo = cnt_lo + jnp.where(sub == 0, jnp.sum(picked[-1], axis=0, keepdims=True), zero)
    cnt_hi = picked[1]
    total = jnp.sum(cnt_lo + cnt_hi, axis=0, keepdims=True)
    tie = jnp.where(total != float(k), one, tie)

    lone = (s1r >= v0[k - 1]) & ((s1r + v1[0]) >= best[k - 1])
    cnt_keys = jnp.where(lone, 1.0, 0.0)
    for a in range(SUBLANES):
        cnt_a = jnp.broadcast_to(cnt_lo[a:a + 1, :], (SUBLANES, LANES))
        cnt_keys = jnp.where(s1r == v0[a], cnt_a, cnt_keys)
    rank1 = jnp.full((tiles, SUBLANES, LANES), float(k), F32)
    for a in range(k - 1, -1, -1):
        rank1 = jnp.where(v1[a] <= s2r, float(a), rank1)
    e2 = jnp.exp(s2r - v1[0])
    c1 = jnp.exp(s1r - v0[0]) / z
    flat = lambda t: t.reshape(nk, LANES)
    return flat(rank1), flat(e2), flat(cnt_keys), flat(c1), tie


def _fold_keys_kernel(keys_ref, wq_ref, o_ref):
    o_ref[...] = lax.dot_general(keys_ref[...], wq_ref[...], (((1,), (1,)), ((), ())),
                                 preferred_element_type=F32, precision=HIGHEST).astype(BF16)


def _fold_keys(keys, wq):
    d, qdim = wq.shape
    nk, half = keys.shape[1:]
    return pl.pallas_call(
        _fold_keys_kernel,
        grid=(qdim // half,),
        in_specs=[pl.BlockSpec((None, nk, half), lambda j: (j % 2, 0, 0)),
                  pl.BlockSpec((d, half), lambda j: (0, j))],
        out_specs=pl.BlockSpec((nk, d), lambda j: (j, 0)),
        out_shape=jax.ShapeDtypeStruct((qdim // half * nk, d), BF16),
        compiler_params=_cparams(("arbitrary",)),
        name="fold_keys",
    )(keys, wq)


def _route_kernel(h_ref, nw_ref, sc_ref, sh_ref, ws_ref,
                  r2_ref, e2_ref, cnt_ref, c1_ref, score_ref, x_ref):
    x_ref[...] = (_rmsnorm_rows(h_ref[...], nw_ref[...]) * (1.0 + sc_ref[...]) + sh_ref[...]).astype(BF16)
    n_col = h_ref.shape[0] // LANES
    nk = PEER_N_KEYS
    per_head = 2 * nk

    def head_scores(head):
        rows = pl.ds(pl.multiple_of(head * per_head, per_head), per_head)
        score_ref[rows, :] = lax.dot_general(ws_ref[rows, :], x_ref[...], (((1,), (1,)), ((), ())),
                                             preferred_element_type=F32)

    head_scores(0)

    def body(head, carry):
        rows0 = pl.ds(pl.multiple_of(head * per_head, nk), nk)
        rows1 = pl.ds(pl.multiple_of(head * per_head + nk, nk), nk)
        for tcol in range(n_col):
            lanes = slice(tcol * LANES, (tcol + 1) * LANES)

            def store(rank1, e2, cnt_keys, c1, tcol=tcol):
                r2_ref[head, tcol] = pltpu.bitcast(rank1.astype(BF16), jnp.int32)
                e2_ref[head, tcol] = pltpu.bitcast(e2.astype(BF16), jnp.int32)
                cnt_ref[head, tcol] = cnt_keys
                c1_ref[head, tcol] = c1

            rank1, e2, cnt_keys, c1, tie = _route_fast(score_ref[rows0, lanes], score_ref[rows1, lanes])
            store(rank1, e2, cnt_keys, c1)
            if tcol == 0:
                head_scores(jnp.minimum(head + 1, PEER_HEADS - 1))

            @pl.when(jnp.max(tie) > 0.0)
            def _(lanes=lanes, store=store):
                store(*_route_exact(score_ref[rows0, lanes], score_ref[rows1, lanes]))

        return carry

    lax.fori_loop(0, PEER_HEADS, body, 0)


def _route(h, norm_w, sc, sh, w_score):
    bsz, s, d = h.shape
    tt = PEER_TOK
    sel = lambda rows: pl.BlockSpec((None, PEER_HEADS, tt // LANES, rows, LANES), lambda b, t: (b, 0, t, 0, 0))
    out = lambda rows, dt: jax.ShapeDtypeStruct((bsz, PEER_HEADS, s // LANES, rows, LANES), dt)
    packed = PEER_N_KEYS * 2 // 4
    n_scores = w_score.shape[0]
    return pl.pallas_call(
        _route_kernel,
        grid=(bsz, s // tt),
        in_specs=[pl.BlockSpec((None, tt, d), lambda b, t: (b, t, 0)),
                  pl.BlockSpec((1, d), lambda b, t: (0, 0)),
                  pl.BlockSpec((None, 1, d), lambda b, t: (b, 0, 0)),
                  pl.BlockSpec((None, 1, d), lambda b, t: (b, 0, 0)),
                  pl.BlockSpec((n_scores, d), lambda b, t: (0, 0))],
        out_specs=[sel(packed), sel(packed), sel(PEER_N_KEYS), sel(PEER_N_KEYS)],
        out_shape=[out(packed, jnp.int32), out(packed, jnp.int32),
                   out(PEER_N_KEYS, F32), out(PEER_N_KEYS, F32)],
        scratch_shapes=[pltpu.VMEM((n_scores, tt), F32), pltpu.VMEM((tt, d), BF16)],
        compiler_params=_cparams(("parallel", "parallel")),
        name="route",
    )(h, norm_w.reshape(1, d), sc, sh, w_score)


def _peer_kernel(h_ref, nw_ref, sc_ref, sh_ref, gate_ref, nf_ref,
                 u_first_ref, u_next_ref, vt_prev_ref, vt_last_ref,
                 r2_ref, e2_ref, cnt_ref, c1_ref, o_ref,
                 x_ref, at_even_ref, at_odd_ref, wg_even_ref, wg_odd_ref, acc_ref,
                 *, final_norm, n_blocks):
    c = pl.program_id(2)
    tt, d = h_ref.shape
    nk = PEER_N_KEYS
    ncol = tt // LANES
    gk = PEER_GROUP_KEYS
    dq = d // PEER_TRIPS
    eq = PEER_IBLK * nk // PEER_TRIPS

    def pre_activations(u_packed, at_ref, rows):
        pre = jnp.dot(pltpu.bitcast(u_packed, BF16), x_ref[...], preferred_element_type=F32)
        for tcol in range(ncol):
            at_ref[tcol, rows, :] = pre[:, tcol * LANES:(tcol + 1) * LANES]

    @pl.when(c == 0)
    def _():
        u2 = _rmsnorm_rows(h_ref[...], nw_ref[...]) * (1.0 + sc_ref[...]) + sh_ref[...]
        x_ref[...] = u2.T.astype(BF16)
        acc_ref[...] = jnp.zeros_like(acc_ref)
        wg_odd_ref[...] = jnp.zeros(wg_odd_ref.shape, BF16)
        pre_activations(u_first_ref[...], at_even_ref, slice(None))

    def row_tile(ref, hd, tcol, key):
        return jnp.broadcast_to(ref[hd, tcol, key:key + 1, :], (nk, LANES)).astype(BF16)

    def gates(wg_ref):
        return jnp.concatenate([wg_ref[tcol] for tcol in range(ncol)], axis=1)

    def step(at_ref, at_next_ref, wg_ref, wg_prev_ref):
        def output_slice(q):
            out_rows = pl.ds(pl.multiple_of(q * dq, dq), dq)
            vt_rows = pl.ds(pl.multiple_of(q * (dq // 2), dq // 2), dq // 2)
            acc_ref[out_rows, :] += jnp.dot(pltpu.bitcast(vt_prev_ref[vt_rows, :], BF16), gates(wg_prev_ref),
                                            preferred_element_type=F32)

        def next_slice(q):
            u_rows = pl.ds(pl.multiple_of(q * (eq // 2), eq // 2), eq // 2)
            pre_activations(u_next_ref[u_rows, :], at_next_ref, pl.ds(pl.multiple_of(q * eq, eq), eq))

        cols = ncol // PEER_TRIPS

        def body(q, carry):
            for j in range(cols):
                tcol = q * cols + j
                for k0 in range(0, PEER_IBLK, gk):
                    w = [jnp.zeros((nk, LANES), BF16) for _ in range(gk)]
                    for hd in range(PEER_HEADS):
                        r2 = pltpu.bitcast(r2_ref[hd, tcol], BF16)
                        e2 = pltpu.bitcast(e2_ref[hd, tcol], BF16)
                        for k in range(gk):
                            sel = r2 < row_tile(cnt_ref, hd, tcol, k0 + k)
                            w[k] = w[k] + (jnp.where(sel, e2, jnp.zeros_like(e2))
                                           * row_tile(c1_ref, hd, tcol, k0 + k))
                    for k in range(gk):
                        rows = slice((k0 + k) * nk, (k0 + k + 1) * nk)
                        a = at_ref[tcol, rows, :]
                        act = 0.5 * a * (1.0 + lax.erf(a * np.float32(math.sqrt(0.5))))
                        wg_ref[tcol, rows, :] = w[k] * act.astype(BF16)
            output_slice(q)
            next_slice(q)
            return carry

        for q in range(PEER_TRIPS):
            body(q, 0)

    @pl.when(c % 2 == 0)
    def _():
        step(at_even_ref, at_odd_ref, wg_even_ref, wg_odd_ref)

    @pl.when(c % 2 == 1)
    def _():
        step(at_odd_ref, at_even_ref, wg_odd_ref, wg_even_ref)

    @pl.when(c == n_blocks - 1)
    def _():
        wg_last_ref = wg_odd_ref if (n_blocks - 1) % 2 else wg_even_ref
        total = acc_ref[...] + jnp.dot(pltpu.bitcast(vt_last_ref[...], BF16), gates(wg_last_ref),
                                       preferred_element_type=F32)
        hh = h_ref[...] + gate_ref[...] * total.T
        o_ref[...] = _rmsnorm_rows(hh, nf_ref[...]) if final_norm else hh


def _peer(h, norm_w, sc, sh, gate2, normf_w, u_bf, vt_bf, r2, e2, cnt, c1, final_norm):
    bsz, s, d = h.shape
    tt = PEER_TOK
    ec = PEER_IBLK * PEER_N_KEYS
    n_exp = 2 * u_bf.shape[0]
    vec = pl.BlockSpec((1, d), lambda b, t, c: (0, 0))
    bvec = pl.BlockSpec((None, 1, d), lambda b, t, c: (b, 0, 0))
    ncol = tt // LANES
    selj = pl.BlockSpec((None, PEER_HEADS, ncol, PEER_N_KEYS // 2, LANES), lambda b, t, c: (b, 0, t, 0, 0))
    seli = pl.BlockSpec((None, PEER_HEADS, ncol, PEER_IBLK, LANES), lambda b, t, c: (b, 0, t, c, 0))
    n_blocks = n_exp // ec
    return pl.pallas_call(
        functools.partial(_peer_kernel, final_norm=final_norm, n_blocks=n_blocks),
        grid=(bsz, s // tt, n_blocks),
        in_specs=[pl.BlockSpec((None, tt, d), lambda b, t, c: (b, t, 0)),
                  vec, bvec, bvec, bvec, vec,
                  pl.BlockSpec((ec // 2, d), lambda b, t, c: (0, 0)),
                  pl.BlockSpec((ec // 2, d), lambda b, t, c: (jnp.minimum(c + 1, n_blocks - 1), 0)),
                  pl.BlockSpec((d // 2, ec), lambda b, t, c: (0, jnp.maximum(c - 1, 0))),
                  pl.BlockSpec((d // 2, ec), lambda b, t, c: (0, n_blocks - 1)),
                  selj, selj, seli, seli],
        out_specs=pl.BlockSpec((None, tt, d), lambda b, t, c: (b, t, 0)),
        out_shape=jax.ShapeDtypeStruct((bsz, s, d), F32),
        scratch_shapes=[pltpu.VMEM((d, tt), BF16),
                        pltpu.VMEM((ncol, ec, LANES), F32),
                        pltpu.VMEM((ncol, ec, LANES), F32),
                        pltpu.VMEM((ncol, ec, LANES), BF16),
                        pltpu.VMEM((ncol, ec, LANES), BF16),
                        pltpu.VMEM((d, tt), F32)],
        compiler_params=_cparams(("parallel", "parallel", "arbitrary")),
        name="peer",
    )(h, norm_w.reshape(1, d), sc, sh, gate2, normf_w.reshape(1, d), u_bf, u_bf, vt_bf, vt_bf,
      r2, e2, cnt, c1)


def _rope_tables(s):
    half = ATT_HEAD_DIM // 2
    inv_freq = ROPE_THETA ** (-jnp.arange(half, dtype=F32) * 2.0 / ATT_HEAD_DIM)
    ang = jnp.arange(s, dtype=F32)[:, None] * inv_freq[None, :]
    cos, sin = jnp.cos(ang), jnp.sin(ang)
    reps = LANES // ATT_HEAD_DIM
    cos_t = jnp.tile(jnp.concatenate([cos, cos], axis=-1), (1, reps))
    sin_t = jnp.tile(jnp.concatenate([-sin, sin], axis=-1), (1, reps))
    return cos_t, sin_t


def _pack_kernel(x_ref, o_ref, *, transpose):
    x = x_ref[...]
    if transpose:
        x = x.T
    o_ref[...] = pltpu.bitcast(x.astype(BF16), jnp.int32)


def _pack_table(x, transpose):
    n, d = x.shape
    blk = PEER_IBLK * PEER_N_KEYS
    if transpose:
        out_spec = pl.BlockSpec((d // 2, blk), lambda j: (0, j))
        out_shape = jax.ShapeDtypeStruct((d // 2, n), jnp.int32)
    else:
        out_spec = pl.BlockSpec((blk // 2, d), lambda j: (j, 0))
        out_shape = jax.ShapeDtypeStruct((n // 2, d), jnp.int32)
    return pl.pallas_call(
        functools.partial(_pack_kernel, transpose=transpose),
        grid=(n // blk,),
        in_specs=[pl.BlockSpec((blk, d), lambda j: (j, 0))],
        out_specs=out_spec,
        out_shape=out_shape,
        compiler_params=_cparams(("parallel",)),
        name="pack_vt" if transpose else "pack_u",
    )(x)


def _permute_in_proj(w_in):
    n_if = 2 * MLSTM_HEADS
    lo = ATT_COLS + MQK_COLS + 2 * MLSTM_W
    pad = jnp.zeros((w_in.shape[0], IF_COLS - n_if), w_in.dtype)
    return jnp.concatenate([w_in[:, :lo], w_in[:, lo + n_if:], w_in[:, lo:lo + n_if], pad], axis=1)


def kernel(x, c, w_ada, b_ada, norm1_w, w_in, conv_w, att_sinks, i_bias, f_bias, mlstm_norm_w,
           w_att_branch, w_mlstm_branch, w_out, norm2_w, peer_w_query, peer_sub_keys, peer_u,
           peer_v, norm_f_w):
    depth = w_ada.shape[0]
    bsz, s, d = x.shape
    cos_t, sin_t = _rope_tables(s)
    h = x
    for l in range(depth):
        mod = _ada(c, w_ada[l], b_ada[l]).reshape(bsz, 6, 1, d)
        sh1, sc1, g1, sh2, sc2, g2 = (mod[:, i] for i in range(6))

        w_perm = _permute_in_proj(w_in[l]).astype(BF16)
        att_in, mqk, mv, mo, gates, gif = _inproj(h, norm1_w[l], sc1, sh1, w_perm)
        att = _attention(att_in, att_sinks[l], cos_t, sin_t)
        bias_row = jnp.concatenate(
            [i_bias[l], f_bias[l], jnp.zeros((IF_COLS - 2 * MLSTM_HEADS,), F32)]).reshape(1, IF_COLS)
        mls = _mlstm(mqk, mv, mo, gif, conv_w[l], bias_row, mlstm_norm_w[l])
        h = _merge(h, att, mls, gates, g1, w_att_branch[l].astype(BF16),
                   w_mlstm_branch[l].astype(BF16), w_out[l].astype(BF16))

        w_score = _fold_keys(peer_sub_keys[l], peer_w_query[l])
        r2, e2, cnt, c1 = _route(h, norm2_w[l], sc2, sh2, w_score)
        h = _peer(h, norm2_w[l], sc2, sh2, g2, norm_f_w, _pack_table(peer_u[l], transpose=False),
                  _pack_table(peer_v[l], transpose=True), r2, e2, cnt, c1,
                  final_norm=(l == depth - 1))
    return h
```
